```python
import math
import jax
import jax.numpy as jnp
from jax import lax
import numpy as np


D_MODEL = 1024
BATCH = 8
SEQ = 8192
DEPTH = 2

GRID_W = 64
CTX_LEN = 256
EPS = 1e-6

MIX_WIDTH = 1024
GMLP_CHUNK = 128
GMLP_GROUPS = 4
GMLP_GROUP_DIM = 128
GMLP_WIDTH = GMLP_GROUPS * GMLP_GROUP_DIM
HGRN_HEADS = 4
HGRN_EXPAND = 128
HGRN_HEAD_V = 128
HGRN_WIDTH = HGRN_HEADS * HGRN_HEAD_V
EVEN_IN = 2 * GMLP_WIDTH + 5 * HGRN_WIDTH
GDN_HEADS = 4
GDN_HEAD_DIM = 128
GDN_WIDTH = GDN_HEADS * GDN_HEAD_DIM
GDN_CONV = 5
MLA_HEADS = 8
MLA_NOPE = 64
MLA_ROPE = 32
MLA_V = 64
MLA_Q_RANK = 256
MLA_KV_RANK = 128
MLA_WIDTH = MLA_HEADS * MLA_V
MLA_SCALE = (MLA_NOPE + MLA_ROPE) ** -0.5
ODD_IN = 4 * GDN_WIDTH + 4 * GDN_HEADS + MLA_Q_RANK + MLA_KV_RANK + MLA_ROPE
ROPE_THETA = 10000.0
ATTN_BLOCK = 128
SCAN_CHUNK = 64
N_GROUPS = 4
EXPERTS_PER_GROUP = 8
N_EXPERTS = N_GROUPS * EXPERTS_PER_GROUP
TOP_K = 2
EXPERT_FF = 512
MOE_BLOCK = 256

kernel_name = 'hybrid_gmlp_hgrn2_gdn_mla_hmoe_trunk'


def rms_norm(x, w):
    xf = x.astype(jnp.float32)
    y = xf * lax.rsqrt(jnp.mean(xf * xf, axis=-1, keepdims=True) + EPS)
    return (y * w.astype(jnp.float32)).astype(x.dtype)


def l2_norm(x):
    xf = x.astype(jnp.float32)
    return (xf * lax.rsqrt(jnp.sum(xf * xf, axis=-1, keepdims=True) + EPS)).astype(x.dtype)


def modulate(h, shift, scale):
    return h * (1 + scale) + shift


def to_heads(t, n_heads):
    b, l, _ = t.shape
    return t.reshape(b, l, n_heads, -1).transpose(0, 2, 1, 3)


def from_heads(t):
    b, n, l, d = t.shape
    return t.transpose(0, 2, 1, 3).reshape(b, l, n * d)


def flip(t):
    return jnp.flip(t, axis=2)


def axial_rope(t):
    l = t.shape[1]
    rows = l // GRID_W
    row = jnp.repeat(jnp.arange(rows), GRID_W)
    col = jnp.tile(jnp.arange(GRID_W), rows)
    half = MLA_ROPE // 2
    quarter = half // 2
    freqs = ROPE_THETA ** (-jnp.arange(quarter, dtype=jnp.float32) / quarter)

    def rot(u, pos):
        ang = pos.astype(jnp.float32)[:, None] * freqs
        cos = jnp.cos(ang)[:, None, :].astype(u.dtype)
        sin = jnp.sin(ang)[:, None, :].astype(u.dtype)
        u1, u2 = u[..., :quarter], u[..., quarter:]
        return jnp.concatenate([u1 * cos - u2 * sin, u2 * cos + u1 * sin], axis=-1)

    return jnp.concatenate([rot(t[..., :half], row), rot(t[..., half:], col)], axis=-1)


def short_conv(t, w):
    pad = (GDN_CONV - 1) // 2
    return lax.conv_general_dilated(t, w[:, None, :].astype(t.dtype), window_strides=(1,),
                                    padding=[(pad, pad)], dimension_numbers=('NWC', 'WIO', 'NWC'),
                                    feature_group_count=t.shape[-1])


def gla_chunked(q, k, v, log_f, s0):
    out_dtype = v.dtype
    b, h, l, _ = q.shape
    dv = v.shape[-1]
    n = l // SCAN_CHUNK

    def chunks(t):
        return jnp.moveaxis(t.astype(jnp.float32).reshape(b, h, n, SCAN_CHUNK, t.shape[-1]), 2, 0)

    causal = jnp.tril(jnp.ones((SCAN_CHUNK, SCAN_CHUNK), bool))

    def step(s, inp):
        qc, kc, vc, lf = inp
        g = jnp.cumsum(lf, axis=-2)
        diff = g[..., :, None, :] - g[..., None, :, :]
        decay = jnp.exp(jnp.where(causal[:, :, None], diff, -jnp.inf))
        scores = jnp.einsum('bhid,bhjd,bhijd->bhij', qc, kc, decay)
        o = scores @ vc + (qc * jnp.exp(g)) @ s
        g_last = g[..., -1:, :]
        s_new = jnp.exp(g_last[..., 0, :])[..., None] * s + jnp.einsum('bhjd,bhje->bhde', kc * jnp.exp(g_last - g), vc)
        return s_new, o

    s_fin, o = lax.scan(step, s0.astype(jnp.float32), (chunks(q), chunks(k), chunks(v), chunks(log_f)))
    return jnp.moveaxis(o, 0, 2).reshape(b, h, l, dv).astype(out_dtype), s_fin


def gated_delta_chunked(q, k, v, log_a, beta, s0):
    out_dtype = v.dtype
    b, h, l, dk = q.shape
    dv = v.shape[-1]
    cs = SCAN_CHUNK
    n = l // cs
    qc = q.astype(jnp.float32).reshape(b, h, n, cs, dk)
    kc = k.astype(jnp.float32).reshape(b, h, n, cs, dk)
    vc = v.astype(jnp.float32).reshape(b, h, n, cs, dv)
    bc = beta.astype(jnp.float32).reshape(b, h, n, cs)
    g = jnp.cumsum(log_a.astype(jnp.float32).reshape(b, h, n, cs), axis=-1)
    incl = jnp.tril(jnp.ones((cs, cs), bool))
    strict = jnp.tril(jnp.ones((cs, cs), bool), -1)
    gamma = jnp.exp(jnp.where(incl, g[..., :, None] - g[..., None, :], -jnp.inf))
    kk = jnp.einsum('bhnid,bhnjd->bhnij', kc, kc)
    tri_mat = jnp.where(strict, bc[..., :, None] * kk * gamma, 0.0) + jnp.eye(cs, dtype=jnp.float32)

    def solve(rhs):
        return lax.linalg.triangular_solve(tri_mat, rhs, left_side=True, lower=True)

    u = solve(bc[..., None] * vc)
    w = solve(bc[..., None] * kc * jnp.exp(g)[..., None])
    qk = jnp.where(incl, jnp.einsum('bhnid,bhnjd->bhnij', qc, kc) * gamma, 0.0)
    qg = qc * jnp.exp(g)[..., None]
    kg = kc * jnp.exp(g[..., -1:] - g)[..., None]
    a_last = jnp.exp(g[..., -1])

    def step(s, inp):
        u_c, w_c, qk_c, qg_c, kg_c, a_c = inp
        v_new = u_c - w_c @ s
        o = qg_c @ s + qk_c @ v_new
        s_new = a_c[..., None, None] * s + jnp.swapaxes(kg_c, -1, -2) @ v_new
        return s_new, o

    xs = tuple(jnp.moveaxis(t, 2, 0) for t in (u, w, qk, qg, kg, a_last))
    s_fin, o = lax.scan(step, s0.astype(jnp.float32), xs)
    return jnp.moveaxis(o, 0, 2).reshape(b, h, l, dv).astype(out_dtype), s_fin


def two_way(scan_fn, ctx_fwd, lat_fwd, ctx_bwd, lat_bwd, s0):
    oc_f, sc_f = scan_fn(*ctx_fwd, s0)
    ox_f, _ = scan_fn(*lat_fwd, sc_f)
    oc_b, sc_b = scan_fn(*[flip(t) for t in ctx_bwd], s0)
    ox_b, _ = scan_fn(*[flip(t) for t in lat_bwd], sc_b)
    return oc_f + flip(oc_b), ox_f + flip(ox_b)


def lower_bound(logits, layer):
    return jnp.cumsum(jax.nn.softmax(logits.astype(jnp.float32), axis=0), axis=0)[layer]


def hgrn_gates(f_raw, lb):
    f = lb + (1.0 - lb) * jax.nn.sigmoid(f_raw.astype(jnp.float32))
    return jnp.log(f), 1.0 - f


def chunk_gmlp(p, norm_w, ws, bs):
    b, l, _ = p.shape
    z = jax.nn.gelu(p)
    u = z[..., :GMLP_WIDTH]
    v = rms_norm(z[..., GMLP_WIDTH:], norm_w)
    v = v.reshape(b, l // GMLP_CHUNK, GMLP_CHUNK, GMLP_GROUPS, GMLP_GROUP_DIM)
    s = jnp.einsum('gpq,bnqgc->bnpgc', ws, v) + bs.T[:, :, None]
    return u * s.reshape(b, l, GMLP_WIDTH)


def attend(q, k, v):
    s = jnp.einsum('bhqd,bhkd->bhqk', q, k, preferred_element_type=jnp.float32) * MLA_SCALE
    p = jax.nn.softmax(s, axis=-1)
    return jnp.einsum('bhqk,bhkd->bhqd', p.astype(v.dtype), v)


def blocked_attend(q, k, v):
    b, h, l, d = q.shape
    nb = l // ATTN_BLOCK
    qb = jnp.moveaxis(q.reshape(b, h, nb, ATTN_BLOCK, d), 2, 0)
    ob = lax.map(lambda qq: attend(qq, k, v), qb)
    return jnp.moveaxis(ob, 0, 2).reshape(b, h, l, v.shape[-1])


def mla_queries(cq, norm_w, w_up, qk_w, rotary):
    b, l, _ = cq.shape
    q = (rms_norm(cq, norm_w) @ w_up).reshape(b, l, MLA_HEADS, MLA_NOPE + MLA_ROPE)
    q = rms_norm(q, qk_w)
    if rotary:
        q = jnp.concatenate([q[..., :MLA_NOPE], axial_rope(q[..., MLA_NOPE:])], axis=-1)
    return q.transpose(0, 2, 1, 3)


def mla_keys_values(ckv, k_rope, norm_w, w_up, qk_w, rotary):
    b, l, _ = ckv.shape
    kv = (rms_norm(ckv, norm_w) @ w_up).reshape(b, l, MLA_HEADS, MLA_NOPE + MLA_V)
    k_shared = jnp.broadcast_to(k_rope[:, :, None, :], (b, l, MLA_HEADS, MLA_ROPE))
    k = rms_norm(jnp.concatenate([kv[..., :MLA_NOPE], k_shared], axis=-1), qk_w)
    if rotary:
        k = jnp.concatenate([k[..., :MLA_NOPE], axial_rope(k[..., MLA_NOPE:])], axis=-1)
    return k.transpose(0, 2, 1, 3), kv[..., MLA_NOPE:].transpose(0, 2, 1, 3)


def even_mixer(hx, hc, layer, need_ctx, w_in, w_out, norm_w, ws, bs, lb_logits, out_norm_w):
    b = hx.shape[0]
    px = hx @ w_in
    pc = hc @ w_in
    lb_f = lower_bound(lb_logits[0], layer)
    lb_b = lower_bound(lb_logits[1], layer)
    a_cols = 2 * GMLP_WIDTH

    def hgrn_inputs(p):
        q, i, f_f, f_b, g = jnp.split(p[..., a_cols:], 5, axis=-1)
        logf_f, k_f = hgrn_gates(f_f, lb_f)
        logf_b, k_b = hgrn_gates(f_b, lb_b)
        qh = to_heads(q, HGRN_HEADS)
        ih = to_heads(i, HGRN_HEADS)
        fwd = (qh, to_heads(k_f, HGRN_HEADS), ih, to_heads(logf_f, HGRN_HEADS))
        bwd = (qh, to_heads(k_b, HGRN_HEADS), ih, to_heads(logf_b, HGRN_HEADS))
        return fwd, bwd, g

    cf, cb, gc = hgrn_inputs(pc)
    xf, xb, gx = hgrn_inputs(px)
    s0 = jnp.zeros((b, HGRN_HEADS, HGRN_EXPAND, HGRN_HEAD_V), jnp.float32)
    oc, ox = two_way(gla_chunked, cf, xf, cb, xb, s0)

    def merge(p, o, g):
        rec = rms_norm(o.transpose(0, 2, 1, 3), out_norm_w.reshape(HGRN_HEADS, HGRN_HEAD_V))
        rec = rec.reshape(g.shape) * jax.nn.silu(g)
        mix = chunk_gmlp(p[..., :a_cols], norm_w, ws, bs)
        return jnp.concatenate([mix, rec], axis=-1) @ w_out

    yx = merge(px, ox, gx)
    yc = merge(pc, oc, gc) if need_ctx else None
    return yx, yc


def odd_mixer(hx, hc, need_ctx, w_in, w_out, conv_w, a_log, dt_bias, gdn_norm_w,
              q_norm_w, wq_up, kv_norm_w, wkv_up, qk_q, qk_k):
    b = hx.shape[0]
    wd, nh = GDN_WIDTH, GDN_HEADS
    o_gate = 3 * wd
    o_beta = 4 * wd
    o_decay = o_beta + 2 * nh
    o_cq = o_decay + 2 * nh
    o_ckv = o_cq + MLA_Q_RANK
    o_kr = o_ckv + MLA_KV_RANK
    px = hx @ w_in
    pc = hc @ w_in
    a_neg = -jnp.exp(a_log.astype(jnp.float32)).reshape(2 * nh)
    dt_b = dt_bias.astype(jnp.float32).reshape(2 * nh)

    def gdn_inputs(p):
        qkv = jax.nn.silu(short_conv(p[..., :3 * wd], conv_w))
        q = l2_norm(to_heads(qkv[..., :wd], nh)) * (GDN_HEAD_DIM ** -0.5)
        k = l2_norm(to_heads(qkv[..., wd:2 * wd], nh))
        v = to_heads(qkv[..., 2 * wd:], nh)
        beta = jax.nn.sigmoid(p[..., o_beta:o_decay].astype(jnp.float32)).transpose(0, 2, 1)
        log_a = (a_neg * jax.nn.softplus(p[..., o_decay:o_cq].astype(jnp.float32) + dt_b)).transpose(0, 2, 1)
        fwd = (q, k, v, log_a[:, :nh], beta[:, :nh])
        bwd = (q, k, v, log_a[:, nh:], beta[:, nh:])
        return fwd, bwd

    cf, cb = gdn_inputs(pc)
    xf, xb = gdn_inputs(px)
    s0 = jnp.zeros((b, nh, GDN_HEAD_DIM, GDN_HEAD_DIM), jnp.float32)
    oc, ox = two_way(gated_delta_chunked, cf, xf, cb, xb, s0)

    def gdn_out(o, p):
        z = p[..., o_gate:o_beta]
        return rms_norm(o.transpose(0, 2, 1, 3), gdn_norm_w).reshape(z.shape) * jax.nn.silu(z)

    kx, vx = mla_keys_values(px[..., o_ckv:o_kr], px[..., o_kr:], kv_norm_w, wkv_up, qk_k, True)
    kc, vc = mla_keys_values(pc[..., o_ckv:o_kr], pc[..., o_kr:], kv_norm_w, wkv_up, qk_k, False)
    qx = mla_queries(px[..., o_cq:o_ckv], q_norm_w, wq_up, qk_q, True)
    k_all = jnp.concatenate([kc, kx], axis=2)
    v_all = jnp.concatenate([vc, vx], axis=2)
    att_x = blocked_attend(qx, k_all, v_all)
    yx = jnp.concatenate([gdn_out(ox, px), from_heads(att_x)], axis=-1) @ w_out
    if not need_ctx:
        return yx, None
    qc = mla_queries(pc[..., o_cq:o_ckv], q_norm_w, wq_up, qk_q, False)
    att_c = attend(qc, kc, vc)
    yc = jnp.concatenate([gdn_out(oc, pc), from_heads(att_c)], axis=-1) @ w_out
    return yx, yc


def hier_moe(h, wg_r, bg_r, we_r, be_r, w_gate, w_up, w_down):
    t, d = h.shape
    hf = h.astype(jnp.float32)
    pg = jax.nn.softmax(hf @ wg_r.astype(jnp.float32) + bg_r.astype(jnp.float32), axis=-1)
    pg_top, g_idx = lax.top_k(pg, 1)
    le = (hf @ we_r.astype(jnp.float32) + be_r.astype(jnp.float32)).reshape(t, N_GROUPS, EXPERTS_PER_GROUP)
    le = jnp.take_along_axis(le, g_idx[:, :, None], axis=1)[:, 0]
    pe_top, e_local = lax.top_k(jax.nn.softmax(le, axis=-1), TOP_K)
    wts = pg_top * pe_top / jnp.sum(pe_top, axis=-1, keepdims=True)
    e_idx = g_idx * EXPERTS_PER_GROUP + e_local
    m = t * TOP_K
    e_flat = e_idx.reshape(m)
    w_flat = wts.reshape(m)
    tok_flat = jnp.repeat(jnp.arange(t, dtype=jnp.int32), TOP_K)
    order = jnp.argsort(e_flat, stable=True)
    e_s, tok_s, w_s = e_flat[order], tok_flat[order], w_flat[order]
    counts = jnp.bincount(e_flat, length=N_EXPERTS)
    padded = (counts + MOE_BLOCK - 1) // MOE_BLOCK * MOE_BLOCK
    start = jnp.cumsum(counts) - counts
    pend = jnp.cumsum(padded)
    pstart = pend - padded
    dest = pstart[e_s] + jnp.arange(m, dtype=jnp.int32) - start[e_s]
    n_blk = -(-m // MOE_BLOCK) + N_EXPERTS
    slots = n_blk * MOE_BLOCK
    slot_tok = jnp.full((slots,), t, jnp.int32).at[dest].set(tok_s)
    slot_w = jnp.zeros((slots,), jnp.float32).at[dest].set(w_s)
    blk_e = jnp.minimum(jnp.searchsorted(pend, jnp.arange(n_blk, dtype=jnp.int32) * MOE_BLOCK, side='right'),
                        N_EXPERTS - 1)
    xs = jnp.concatenate([h, jnp.zeros((1, d), h.dtype)], axis=0)[slot_tok].reshape(n_blk, MOE_BLOCK, d)

    def expert(args):
        xb, e = args
        return (jax.nn.silu(xb @ w_gate[e]) * (xb @ w_up[e])) @ w_down[e]

    ys = lax.map(expert, (xs, blk_e)).reshape(slots, d)
    ys = ys * slot_w[:, None].astype(ys.dtype)
    return jax.ops.segment_sum(ys, slot_tok, num_segments=t + 1)[:t]


def setup_inputs(seed: int = 0) -> dict:
    key = jax.random.key(seed)
    ks = iter(jax.random.split(key, 48))
    d = D_MODEL
    ne = (DEPTH + 1) // 2
    no = DEPTH // 2

    def nrm(shape, scale):
        return jax.random.normal(next(ks), shape, jnp.float32) * scale

    dt = jnp.exp(jax.random.uniform(next(ks), (no, 2, GDN_HEADS), jnp.float32,
                                    minval=math.log(1e-3), maxval=math.log(1e-1)))
    a_init = jax.random.uniform(next(ks), (no, 2, GDN_HEADS), jnp.float32, minval=1.0, maxval=16.0)
    return {
        'x': nrm((BATCH, SEQ, d), 1.0),
        'c': nrm((BATCH, d), 1.0),
        'ctx': nrm((BATCH, CTX_LEN, d), 1.0),
        'c_ctx': nrm((d,), 1.0),
        'ada_w': nrm((DEPTH, d, 6 * d), 0.5 * d ** -0.5),
        'ada_b': nrm((DEPTH, 6 * d), 0.01),
        'norm_mix_w': 1.0 + nrm((DEPTH, d), 0.01),
        'norm_ffn_w': 1.0 + nrm((DEPTH, d), 0.01),
        'even_w_in': nrm((ne, d, EVEN_IN), d ** -0.5),
        'even_w_out': nrm((ne, MIX_WIDTH, d), MIX_WIDTH ** -0.5),
        'gmlp_norm_w': 1.0 + nrm((ne, GMLP_WIDTH), 0.01),
        'gmlp_ws': nrm((ne, GMLP_GROUPS, GMLP_CHUNK, GMLP_CHUNK), GMLP_CHUNK ** -0.5),
        'gmlp_bs': 1.0 + nrm((ne, GMLP_GROUPS, GMLP_CHUNK), 0.01),
        'hgrn_lb_logits': nrm((2, DEPTH + 1, HGRN_WIDTH), 0.5),
        'hgrn_norm_w': 1.0 + nrm((ne, HGRN_WIDTH), 0.01),
        'odd_w_in': nrm((no, d, ODD_IN), d ** -0.5),
        'odd_w_out': nrm((no, MIX_WIDTH, d), MIX_WIDTH ** -0.5),
        'gdn_conv_w': nrm((no, GDN_CONV, 3 * GDN_WIDTH), GDN_CONV ** -0.5),
        'gdn_a_log': jnp.log(a_init),
        'gdn_dt_bias': dt + jnp.log(-jnp.expm1(-dt)),
        'gdn_norm_w': 1.0 + nrm((no, GDN_HEAD_DIM), 0.01),
        'mla_q_norm_w': 1.0 + nrm((no, MLA_Q_RANK), 0.01),
        'mla_wq_up': nrm((no, MLA_Q_RANK, MLA_HEADS * (MLA_NOPE + MLA_ROPE)), MLA_Q_RANK ** -0.5),
        'mla_kv_norm_w': 1.0 + nrm((no, MLA_KV_RANK), 0.01),
        'mla_wkv_up': nrm((no, MLA_KV_RANK, MLA_HEADS * (MLA_NOPE + MLA_V)), MLA_KV_RANK ** -0.5),
        'mla_qk_norm_q': 1.0 + nrm((no, MLA_NOPE + MLA_ROPE), 0.01),
        'mla_qk_norm_k': 1.0 + nrm((no, MLA_NOPE + MLA_ROPE), 0.01),
        'router_group_w': nrm((DEPTH, d, N_GROUPS), d ** -0.5),
        'router_group_b': nrm((DEPTH, N_GROUPS), 0.01),
        'router_expert_w': nrm((DEPTH, d, N_EXPERTS), d ** -0.5),
        'router_expert_b': nrm((DEPTH, N_EXPERTS), 0.01),
        'moe_w_gate': nrm((DEPTH, N_EXPERTS, d, EXPERT_FF), d ** -0.5),
        'moe_w_up': nrm((DEPTH, N_EXPERTS, d, EXPERT_FF), d ** -0.5),
        'moe_w_down': nrm((DEPTH, N_EXPERTS, EXPERT_FF, d), EXPERT_FF ** -0.5),
    }


def reference(x, c, ctx, c_ctx, ada_w, ada_b, norm_mix_w, norm_ffn_w,
              even_w_in, even_w_out, gmlp_norm_w, gmlp_ws, gmlp_bs, hgrn_lb_logits, hgrn_norm_w,
              odd_w_in, odd_w_out, gdn_conv_w, gdn_a_log, gdn_dt_bias, gdn_norm_w,
              mla_q_norm_w, mla_wq_up, mla_kv_norm_w, mla_wkv_up, mla_qk_norm_q, mla_qk_norm_k,
              router_group_w, router_group_b, router_expert_w, router_expert_b,
              moe_w_gate, moe_w_up, moe_w_down):
    b, n_lat, d = x.shape
    xc = ctx
    silu_c = jax.nn.silu(c)
    silu_cc = jax.nn.silu(c_ctx)
    for layer in range(DEPTH):
        need_ctx = layer < DEPTH - 1
        j = layer // 2
        mx = jnp.split((silu_c @ ada_w[layer] + ada_b[layer])[:, None, :], 6, axis=-1)
        mc = jnp.split(silu_cc @ ada_w[layer] + ada_b[layer], 6, axis=-1)
        hx = modulate(rms_norm(x, norm_mix_w[layer]), mx[0], mx[1])
        hc = modulate(rms_norm(xc, norm_mix_w[layer]), mc[0], mc[1])
        if layer % 2 == 0:
            yx, yc = even_mixer(hx, hc, layer, need_ctx, even_w_in[j], even_w_out[j], gmlp_norm_w[j],
                                gmlp_ws[j], gmlp_bs[j], hgrn_lb_logits, hgrn_norm_w[j])
        else:
            yx, yc = odd_mixer(hx, hc, need_ctx, odd_w_in[j], odd_w_out[j], gdn_conv_w[j], gdn_a_log[j],
                               gdn_dt_bias[j], gdn_norm_w[j], mla_q_norm_w[j], mla_wq_up[j],
                               mla_kv_norm_w[j], mla_wkv_up[j], mla_qk_norm_q[j], mla_qk_norm_k[j])
        x = x + mx[2] * yx
        moe_params = (router_group_w[layer], router_group_b[layer], router_expert_w[layer],
                      router_expert_b[layer], moe_w_gate[layer], moe_w_up[layer], moe_w_down[layer])
        hx2 = modulate(rms_norm(x, norm_ffn_w[layer]), mx[3], mx[4])
        if need_ctx:
            xc = xc + mc[2] * yc
            hc2 = modulate(rms_norm(xc, norm_ffn_w[layer]), mc[3], mc[4])
            tokens = jnp.concatenate([hx2.reshape(-1, d), hc2.reshape(-1, d)], axis=0)
            y = hier_moe(tokens, *moe_params)
            x = x + mx[5] * y[:b * n_lat].reshape(x.shape)
            xc = xc + mc[5] * y[b * n_lat:].reshape(xc.shape)
        else:
            x = x + mx[5] * hier_moe(hx2.reshape(-1, d), *moe_params).reshape(x.shape)
    return x
```

```python
import functools
import math

import jax
import jax.numpy as jnp
import numpy as np
from jax import lax
from jax.experimental import pallas as pl
from jax.experimental.pallas import tpu as pltpu

F32 = jnp.float32
BF16 = jnp.bfloat16
EPS = 1e-6

LANES = 128
SUBLANES = 8
TOKEN_TILE = 256
CHUNK = 128
VMEM_LIMIT = 56 * 1024 * 1024

GRID_W = 64
ROPE_THETA = 10000.0
HEAD_DIM = 128
N_HEADS = 4
MIX_HALF = N_HEADS * HEAD_DIM
GDN_CONV = 5
MLA_HEADS = 8
MLA_NOPE = 64
MLA_ROPE = 32
MLA_V = 64
MLA_QK = MLA_NOPE + MLA_ROPE
MLA_Q_RANK = 256
MLA_KV_RANK = 128
N_GROUPS = 4
EXPERTS_PER_GROUP = 8
N_EXPERTS = N_GROUPS * EXPERTS_PER_GROUP
MOE_BLOCK = 256
ATTN_TQ = 256
ATTN_TK_CANDIDATES = (768, 512, 384, 256, 128)

NT_DIMS = (((1,), (1,)), ((), ()))
TN_DIMS = (((0,), (0,)), ((), ()))


def _cparams(sem):
    return pltpu.CompilerParams(dimension_semantics=sem, vmem_limit_bytes=VMEM_LIMIT)


def _sigmoid(x):
    return 1.0 / (1.0 + jnp.exp(-x))


def _silu(x):
    return x * _sigmoid(x)


def _gelu_tanh(x):
    c = math.sqrt(2.0 / math.pi)
    return 0.5 * x * (1.0 + jnp.tanh(c * (x + 0.044715 * (x * x * x))))


def _softplus(x):
    return jnp.maximum(x, 0.0) + jnp.log(1.0 + jnp.exp(-jnp.abs(x)))


def _bdot(a, b):
    return jnp.dot(a.astype(BF16), b.astype(BF16), preferred_element_type=F32)


def _bdot_nt(a, b):
    return lax.dot_general(a.astype(BF16), b.astype(BF16), NT_DIMS, preferred_element_type=F32)


def _bdot_tn(a, b):
    return lax.dot_general(a.astype(BF16), b.astype(BF16), TN_DIMS, preferred_element_type=F32)


def _split_bf16(x, n):
    parts = []
    r = x
    for _ in range(n):
        p = r.astype(BF16)
        parts.append(p)
        r = r - p.astype(F32)
    return parts


def _dot_exact_lhs(m01, x):
    acc = None
    for p in _split_bf16(x, 3):
        t = jnp.dot(m01, p, preferred_element_type=F32)
        acc = t if acc is None else acc + t
    return acc


def _dot3(a, b):
    a_hi, a_lo = _split_bf16(a, 2)
    b_hi, b_lo = _split_bf16(b, 2)
    d = lambda u, v: jnp.dot(u, v, preferred_element_type=F32)
    return d(a_hi, b_hi) + (d(a_hi, b_lo) + d(a_lo, b_hi))


def _rms(x, w, n=None):
    n = x.shape[-1] if n is None else n
    ms = jnp.sum(x * x, axis=-1, keepdims=True) * (1.0 / n)
    return x * lax.rsqrt(ms + EPS) * w


def _mod_kernel(c_ref, w_ref, b_ref, o_ref):
    s = _silu(c_ref[...])
    o_ref[0] = _dot3(s, w_ref[0]) + b_ref[0]


def _mod_vectors(cvec, ada_w, ada_b):
    depth, d, n = ada_w.shape
    rows = cvec.shape[0]
    tn = 512
    return pl.pallas_call(
        _mod_kernel,
        grid=(depth, n // tn),
        in_specs=[
            pl.BlockSpec((rows, d), lambda l, j: (0, 0)),
            pl.BlockSpec((1, d, tn), lambda l, j: (l, 0, j)),
            pl.BlockSpec((1, 1, tn), lambda l, j: (l, 0, j)),
        ],
        out_specs=pl.BlockSpec((1, rows, tn), lambda l, j: (l, 0, j)),
        out_shape=jax.ShapeDtypeStruct((depth, rows, n), F32),
        compiler_params=_cparams(("parallel", "parallel")),
        name="adaln_mod",
    )(cvec, ada_w, ada_b.reshape(depth, 1, n))


def _mod_index(nct):
    return lambda b, t: (b, jnp.where(t < nct, 0, 1), 0, 0)


def _inproj_kernel(x_ref, mod_ref, nw_ref, w_ref, o_ref, *, shift_row, tn):
    mod = mod_ref[0, 0]
    h = _rms(x_ref[0], nw_ref[...]) * (1.0 + mod[shift_row + 1:shift_row + 2]) + mod[shift_row:shift_row + 1]
    hb = h.astype(BF16)
    for j in range(w_ref.shape[1] // tn):
        o_ref[0, :, j * tn:(j + 1) * tn] = jnp.dot(hb, w_ref[:, j * tn:(j + 1) * tn], preferred_element_type=F32)


def _in_projection(xa, modtab, norm_w, w_bf16, nct):
    b, lt, d = xa.shape
    n = w_bf16.shape[1]
    tm = TOKEN_TILE
    tn = 512 if n % 512 == 0 else 256
    return pl.pallas_call(
        functools.partial(_inproj_kernel, shift_row=0, tn=tn),
        grid=(b, lt // tm),
        in_specs=[
            pl.BlockSpec((1, tm, d), lambda i, t: (i, t, 0)),
            pl.BlockSpec((1, 1, 8, d), _mod_index(nct)),
            pl.BlockSpec((1, d), lambda i, t: (0, 0)),
            pl.BlockSpec((d, n), lambda i, t: (0, 0)),
        ],
        out_specs=pl.BlockSpec((1, tm, n), lambda i, t: (i, t, 0)),
        out_shape=jax.ShapeDtypeStruct((b, lt, n), F32),
        compiler_params=_cparams(("parallel", "parallel")),
        name="in_projection",
    )(xa, modtab, norm_w.reshape(1, d), w_bf16)


def _chunk_order(nc_ctx, nc_all, rev):
    if not rev:
        return lambda s: s
    return lambda s: jnp.where(s < nc_ctx, nc_ctx - 1 - s, nc_all - 1 - (s - nc_ctx))


def _tri01(n, rev):
    r = lax.broadcasted_iota(jnp.int32, (n, n), 0)
    c = lax.broadcasted_iota(jnp.int32, (n, n), 1)
    keep = (c >= r) if rev else (c <= r)
    return jnp.where(keep, 1.0, 0.0).astype(BF16)


def _level_ref_rows(g_ref, hd, half, rev):
    seg = 2 * half
    nseg = CHUNK // seg
    row_of = lambda s: s * seg + half - (0 if rev else 1)
    if seg >= SUBLANES:
        pieces = [jnp.broadcast_to(g_ref[hd, pl.ds(row_of(s), 1), :], (seg, LANES)) for s in range(nseg)]
        return jnp.concatenate(pieces, axis=0)
    sub = lax.broadcasted_iota(jnp.int32, (SUBLANES, LANES), 0) >> int(math.log2(seg))
    per = SUBLANES // seg
    pieces = []
    for v in range(CHUNK // SUBLANES):
        acc = jnp.broadcast_to(g_ref[hd, pl.ds(row_of(v * per), 1), :], (SUBLANES, LANES))
        for j in range(1, per):
            cand = jnp.broadcast_to(g_ref[hd, pl.ds(row_of(v * per + j), 1), :], (SUBLANES, LANES))
            acc = jnp.where(sub >= j, cand, acc)
        pieces.append(acc)
    return jnp.concatenate(pieces, axis=0)


def _hgrn_kernel(q_ref, v_ref, f_ref, lb_ref, o_ref, st_ref, g_ref, *, rev):
    @pl.when(pl.program_id(1) == 0)
    def _():
        st_ref[...] = jnp.zeros_like(st_ref)

    c = CHUNK
    tri = _tri01(c, rev)
    rows = lax.broadcasted_iota(jnp.int32, (c, LANES), 0)
    ri = lax.broadcasted_iota(jnp.int32, (c, c), 0)
    ci = lax.broadcasted_iota(jnp.int32, (c, c), 1)
    n_levels = int(math.log2(c))
    for hd in range(N_HEADS):
        sl = slice(hd * HEAD_DIM, (hd + 1) * HEAD_DIM)
        q = q_ref[0, :, sl]
        v = v_ref[0, :, sl]
        lb = lb_ref[:, sl]
        f = lb + (1.0 - lb) * _sigmoid(f_ref[0, :, sl])
        k = 1.0 - f
        g = _dot_exact_lhs(tri, jnp.log(f))
        g_ref[hd] = g
        a = jnp.where(ri == ci, jnp.sum(q * k, axis=-1, keepdims=True), 0.0)
        for lv in range(n_levels):
            half = 1 << lv
            e = jnp.exp(-jnp.abs(g - _level_ref_rows(g_ref, hd, half, rev)))
            upper = (rows & half) != 0
            is_q = jnp.logical_not(upper) if rev else upper
            qe = jnp.where(is_q, q * e, 0.0)
            ke = jnp.where(is_q, 0.0, k * e)
            same = (ri >> (lv + 1)) == (ci >> (lv + 1))
            a = a + jnp.where(same, _bdot_nt(qe, ke), 0.0)
        g_tot = g_ref[hd, pl.ds(0 if rev else c - 1, 1), :]
        st = st_ref[hd]
        o = _bdot(a, v) + _bdot_nt(q * jnp.exp(g), st)
        st_ref[hd] = st * jnp.exp(g_tot) + _bdot_tn(v, k * jnp.exp(g_tot - g))
        o_ref[0, :, sl] = o


def _hgrn_scan(px, lb, nc_ctx, rev, f_block):
    b, lt, _ = px.shape
    nc = lt // CHUNK
    order = _chunk_order(nc_ctx, nc, rev)
    w = MIX_HALF
    spec = lambda blk: pl.BlockSpec((1, CHUNK, w), lambda i, s: (i, order(s), blk))
    return pl.pallas_call(
        functools.partial(_hgrn_kernel, rev=rev),
        grid=(b, nc),
        in_specs=[spec(2), spec(3), spec(f_block), pl.BlockSpec((1, w), lambda i, s: (0, 0))],
        out_specs=pl.BlockSpec((1, CHUNK, w), lambda i, s: (i, order(s), 0)),
        out_shape=jax.ShapeDtypeStruct((b, lt, w), F32),
        scratch_shapes=[pltpu.VMEM((N_HEADS, HEAD_DIM, HEAD_DIM), F32),
                        pltpu.VMEM((N_HEADS, CHUNK, HEAD_DIM), F32)],
        compiler_params=_cparams(("parallel", "arbitrary")),
        name="hgrn_rev" if rev else "hgrn_fwd",
    )(px, px, px, lb)


def _even_merge_kernel(pa_ref, pg_ref, of_ref, ob_ref, x_ref, mod_ref, gnw_ref, ws_ref, bsb_ref,
                       hnw_ref, wo_ref, o_ref, cat_ref):
    tm = pa_ref.shape[1]
    z = _gelu_tanh(pa_ref[0])
    u = z[:, :MIX_HALF]
    v = _rms(z[:, MIX_HALF:], gnw_ref[...])
    for cc in range(tm // CHUNK):
        rs = slice(cc * CHUNK, (cc + 1) * CHUNK)
        for g in range(N_HEADS):
            cs = slice(g * HEAD_DIM, (g + 1) * HEAD_DIM)
            s = _bdot(ws_ref[g], v[rs, cs]) + bsb_ref[g]
            cat_ref[rs, cs] = (u[rs, cs] * s).astype(BF16)
    o = of_ref[0] + ob_ref[0]
    gate = _silu(pg_ref[0])
    for hd in range(N_HEADS):
        cs = slice(hd * HEAD_DIM, (hd + 1) * HEAD_DIM)
        rec = _rms(o[:, cs], hnw_ref[:, cs]) * gate[:, cs]
        cat_ref[:, MIX_HALF + hd * HEAD_DIM:MIX_HALF + (hd + 1) * HEAD_DIM] = rec.astype(BF16)
    y = jnp.dot(cat_ref[...], wo_ref[...], preferred_element_type=F32)
    o_ref[0] = x_ref[0] + mod_ref[0, 0][2:3] * y


def _even_merge(px, o_f, o_b, xa, modtab, gnw, ws_bf16, bsb, hnw, wo_bf16, nct):
    b, lt, d = xa.shape
    tm = TOKEN_TILE
    w = MIX_HALF
    tile = lambda width, blk: pl.BlockSpec((1, tm, width), lambda i, t: (i, t, blk))
    const = lambda shape: pl.BlockSpec(shape, lambda i, t: (0,) * len(shape))
    return pl.pallas_call(
        _even_merge_kernel,
        grid=(b, lt // tm),
        in_specs=[
            tile(2 * w, 0), tile(w, 6), tile(w, 0), tile(w, 0), tile(d, 0),
            pl.BlockSpec((1, 1, 8, d), _mod_index(nct)),
            const((1, w)), const((N_HEADS, CHUNK, CHUNK)), const((N_HEADS, CHUNK, HEAD_DIM)),
            const((1, w)), const((2 * w, d)),
        ],
        out_specs=tile(d, 0),
        out_shape=jax.ShapeDtypeStruct((b, lt, d), F32),
        scratch_shapes=[pltpu.VMEM((tm, 2 * w), BF16)],
        input_output_aliases={4: 0},
        compiler_params=_cparams(("parallel", "parallel")),
        name="even_merge",
    )(px, px, o_f, o_b, xa, modtab, gnw, ws_bf16, bsb, hnw, wo_bf16)


def _odd_prep_kernel(p_ref, prev_ref, next_ref, cw_ref, gvec_ref, qnw_ref, wq_ref, qkw_ref, kvnw_ref,
                     wkv_ref, kkw_ref, ct_ref, st_ref,
                     qkv_ref, gates_ref, q_ref, k_ref, v_ref, *, nct, ntiles, scale):
    tm = p_ref.shape[1]
    t = pl.program_id(1)
    w3 = 3 * MIX_HALF
    first = jnp.logical_or(t == 0, t == nct)
    last = jnp.logical_or(t == nct - 1, t == ntiles - 1)
    xin = p_ref[0, :, :w3]
    prev8 = jnp.where(first, 0.0, prev_ref[0])
    next8 = jnp.where(last, 0.0, next_ref[0])
    row8 = lax.broadcasted_iota(jnp.int32, (SUBLANES, w3), 0)
    pad = (GDN_CONV - 1) // 2
    acc = xin * cw_ref[pad:pad + 1, :]
    for sh in range(1, pad + 1):
        dn = pltpu.roll(xin, sh, 0)
        top = jnp.where(row8 < sh, pltpu.roll(prev8, sh, 0), dn[:SUBLANES])
        dn = jnp.concatenate([top, dn[SUBLANES:]], axis=0)
        acc = acc + dn * cw_ref[pad - sh:pad - sh + 1, :]
        up = pltpu.roll(xin, tm - sh, 0)
        bot = jnp.where(row8 >= SUBLANES - sh, pltpu.roll(next8, SUBLANES - sh, 0), up[tm - SUBLANES:])
        up = jnp.concatenate([up[:tm - SUBLANES], bot], axis=0)
        acc = acc + up * cw_ref[pad + sh:pad + sh + 1, :]
    act = _silu(acc)
    for hd in range(N_HEADS):
        for part, mul in ((0, HEAD_DIM ** -0.5), (1, 1.0)):
            cs = slice(part * MIX_HALF + hd * HEAD_DIM, part * MIX_HALF + (hd + 1) * HEAD_DIM)
            a = act[:, cs]
            qkv_ref[0, :, cs] = a * lax.rsqrt(jnp.sum(a * a, axis=-1, keepdims=True) + EPS) * mul
    qkv_ref[0, :, 2 * MIX_HALF:] = act[:, 2 * MIX_HALF:]

    gx = p_ref[0, :, 21 * LANES:22 * LANES]
    lane = lax.broadcasted_iota(jnp.int32, (tm, LANES), 1)
    a_neg = -jnp.exp(gvec_ref[0:1, :])
    gates_ref[0] = jnp.where(lane < 2 * N_HEADS, _sigmoid(gx), a_neg * _softplus(gx + gvec_ref[1:2, :]))

    ct = ct_ref[...]
    st = st_ref[...]
    inv_n = 1.0 / MLA_QK
    cq = _rms(p_ref[0, :, 16 * LANES:18 * LANES], qnw_ref[...])
    qraw = jnp.dot(cq.astype(BF16), wq_ref[...], preferred_element_type=F32)
    hw = MLA_HEADS * LANES
    for hd in range(MLA_HEADS):
        qa = qraw[:, hd * LANES:(hd + 1) * LANES]
        qs = qraw[:, hw + hd * LANES:hw + (hd + 1) * LANES]
        rs = lax.rsqrt(jnp.sum(qa * qa, axis=-1, keepdims=True) * inv_n + EPS)
        rot = (qa * rs * qkw_ref[0:1, :]) * ct + (qs * rs * qkw_ref[1:2, :]) * st
        q_ref[0, hd] = (rot * scale).astype(BF16)
    ckv = _rms(p_ref[0, :, 18 * LANES:19 * LANES], kvnw_ref[...])
    kv = jnp.dot(ckv.astype(BF16), wkv_ref[...], preferred_element_type=F32)
    kr = p_ref[0, :, 19 * LANES:20 * LANES]
    krs = p_ref[0, :, 20 * LANES:21 * LANES]
    for hd in range(MLA_HEADS):
        ka = kv[:, hd * LANES:(hd + 1) * LANES] + kr
        rs = lax.rsqrt(jnp.sum(ka * ka, axis=-1, keepdims=True) * inv_n + EPS)
        rot = (ka * rs * kkw_ref[0:1, :]) * ct + (krs * rs * kkw_ref[1:2, :]) * st
        k_ref[0, hd] = rot.astype(BF16)
    v_ref[0] = kv[:, hw:].astype(BF16)


def _odd_prep(px, cw, gvec, qnw, wq, qkw, kvnw, wkv, kkw, ctab, stab, nct):
    b, lt, n = px.shape
    tm = TOKEN_TILE
    ntiles = lt // tm
    w3 = 3 * MIX_HALF
    per = tm // SUBLANES
    nrow8 = lt // SUBLANES
    const = lambda shape: pl.BlockSpec(shape, lambda i, t: (0,) * len(shape))
    hv = MLA_HEADS * MLA_V
    return pl.pallas_call(
        functools.partial(_odd_prep_kernel, nct=nct, ntiles=ntiles, scale=MLA_QK ** -0.5),
        grid=(b, ntiles),
        in_specs=[
            pl.BlockSpec((1, tm, n), lambda i, t: (i, t, 0)),
            pl.BlockSpec((1, SUBLANES, w3), lambda i, t: (i, jnp.maximum(t * per - 1, 0), 0)),
            pl.BlockSpec((1, SUBLANES, w3), lambda i, t: (i, jnp.minimum((t + 1) * per, nrow8 - 1), 0)),
            const((8, w3)), const((8, LANES)), const((1, MLA_Q_RANK)),
            const((MLA_Q_RANK, 2 * MLA_HEADS * LANES)), const((8, LANES)), const((1, MLA_KV_RANK)),
            const((MLA_KV_RANK, MLA_HEADS * LANES + hv)), const((8, LANES)),
            pl.BlockSpec((tm, LANES), lambda i, t: (t, 0)),
            pl.BlockSpec((tm, LANES), lambda i, t: (t, 0)),
        ],
        out_specs=[
            pl.BlockSpec((1, tm, w3), lambda i, t: (i, t, 0)),
            pl.BlockSpec((1, tm, LANES), lambda i, t: (i, t, 0)),
            pl.BlockSpec((1, MLA_HEADS, tm, LANES), lambda i, t: (i, 0, t, 0)),
            pl.BlockSpec((1, MLA_HEADS, tm, LANES), lambda i, t: (i, 0, t, 0)),
            pl.BlockSpec((1, tm, hv), lambda i, t: (i, t, 0)),
        ],
        out_shape=[
            jax.ShapeDtypeStruct((b, lt, w3), F32),
            jax.ShapeDtypeStruct((b, lt, LANES), F32),
            jax.ShapeDtypeStruct((b, MLA_HEADS, lt, LANES), BF16),
            jax.ShapeDtypeStruct((b, MLA_HEADS, lt, LANES), BF16),
            jax.ShapeDtypeStruct((b, lt, hv), BF16),
        ],
        compiler_params=_cparams(("parallel", "parallel")),
        name="odd_prep",
    )(px, px, px, cw, gvec, qnw, wq, qkw, kvnw, wkv, kkw, ctab, stab)


def _unit_tri_inverse(nm, ri, ci, rev):
    c = nm.shape[0]
    x = None
    for lv in range(int(math.log2(c))):
        same = (ri >> (lv + 1)) == (ci >> (lv + 1))
        r_up = ((ri >> lv) & 1) == 1
        c_up = ((ci >> lv) & 1) == 1
        if rev:
            pick = jnp.logical_and(jnp.logical_not(r_up), c_up)
        else:
            pick = jnp.logical_and(r_up, jnp.logical_not(c_up))
        blk = jnp.where(jnp.logical_and(same, pick), nm, 0.0)
        if x is None:
            x = jnp.where(ri == ci, 1.0, 0.0) - blk
        else:
            x = x - _bdot(x, _bdot(blk, x))
    return x


def _gdn_kernel(q_ref, k_ref, v_ref, gt_ref, o_ref, s_ref, *, rev):
    @pl.when(pl.program_id(1) == 0)
    def _():
        s_ref[...] = jnp.zeros_like(s_ref)

    c = CHUNK
    tri = _tri01(c, rev)
    ri = lax.broadcasted_iota(jnp.int32, (c, c), 0)
    ci = lax.broadcasted_iota(jnp.int32, (c, c), 1)
    incl = (ci >= ri) if rev else (ci <= ri)
    strict = (ci > ri) if rev else (ci < ri)
    gt = gt_ref[0]
    d_off = N_HEADS if rev else 0
    for hd in range(N_HEADS):
        sl = slice(hd * HEAD_DIM, (hd + 1) * HEAD_DIM)
        q = q_ref[0, :, sl]
        k = k_ref[0, :, sl]
        v = v_ref[0, :, sl]
        jb = d_off + hd
        beta = jnp.broadcast_to(gt[:, jb:jb + 1], (c, LANES))
        la = jnp.broadcast_to(gt[:, 2 * N_HEADS + jb:2 * N_HEADS + jb + 1], (c, LANES))
        g = _dot_exact_lhs(tri, la)
        gamma = jnp.exp(jnp.where(incl, g - g.T, -jnp.inf))
        kq = _bdot_nt(jnp.concatenate([k, q], axis=0), k)
        nm = jnp.where(strict, beta * kq[:c] * gamma, 0.0)
        x = _unit_tri_inverse(nm, ri, ci, rev)
        eg = jnp.exp(g)
        uw = _bdot(x, jnp.concatenate([beta * v, beta * k * eg], axis=1))
        u = uw[:, :HEAD_DIM]
        w = uw[:, HEAD_DIM:]
        qk = jnp.where(incl, kq[c:] * gamma, 0.0)
        g_tot = g[0:1, :] if rev else g[c - 1:c, :]
        s = s_ref[hd]
        v_new = u - _bdot(w, s)
        o_ref[0, :, sl] = _bdot(q * eg, s) + _bdot(qk, v_new)
        s_ref[hd] = jnp.exp(g_tot) * s + _bdot_tn(k * jnp.exp(g_tot - g), v_new)


def _gdn_scan(qkv, gates, nc_ctx, rev):
    b, lt, _ = qkv.shape
    nc = lt // CHUNK
    order = _chunk_order(nc_ctx, nc, rev)
    w = MIX_HALF
    spec = lambda blk: pl.BlockSpec((1, CHUNK, w), lambda i, s: (i, order(s), blk))
    return pl.pallas_call(
        functools.partial(_gdn_kernel, rev=rev),
        grid=(b, nc),
        in_specs=[spec(0), spec(1), spec(2),
                  pl.BlockSpec((1, CHUNK, LANES), lambda i, s: (i, order(s), 0))],
        out_specs=pl.BlockSpec((1, CHUNK, w), lambda i, s: (i, order(s), 0)),
        out_shape=jax.ShapeDtypeStruct((b, lt, w), F32),
        scratch_shapes=[pltpu.VMEM((N_HEADS, HEAD_DIM, HEAD_DIM), F32)],
        compiler_params=_cparams(("parallel", "arbitrary")),
        name="gdn_rev" if rev else "gdn_fwd",
    )(qkv, qkv, qkv, gates)


def _attn_kernel(q_ref, k_ref, v_ref, o_ref, m_ref, l_ref, acc_ref):
    ki = pl.program_id(3)

    @pl.when(ki == 0)
    def _():
        m_ref[...] = jnp.full_like(m_ref, -jnp.inf)
        l_ref[...] = jnp.zeros_like(l_ref)
        acc_ref[...] = jnp.zeros_like(acc_ref)

    v = v_ref[0]
    for a in range(2):
        s = lax.dot_general(q_ref[0, a], k_ref[0, a], NT_DIMS, preferred_element_type=F32)
        m_prev = m_ref[a]
        m_new = jnp.maximum(m_prev, jnp.max(s, axis=-1, keepdims=True))
        p = jnp.exp(s - m_new[:, :1])
        alpha = jnp.exp(m_prev - m_new)
        l_ref[a] = alpha * l_ref[a] + jnp.sum(p, axis=-1, keepdims=True)
        acc_ref[a] = alpha * acc_ref[a] + jnp.dot(p.astype(BF16), v, preferred_element_type=F32)
        m_ref[a] = m_new

    @pl.when(ki == pl.num_programs(3) - 1)
    def _():
        lane = lax.broadcasted_iota(jnp.int32, acc_ref.shape[1:], 1)
        o_ref[0] = jnp.where(lane < MLA_V, acc_ref[0] / l_ref[0], acc_ref[1] / l_ref[1])


def _attention(q, k, v, n_ctx):
    b, nh, lt, _ = q.shape
    lq = lt - n_ctx
    tq = ATTN_TQ
    tk = next(c for c in ATTN_TK_CANDIDATES if lt % c == 0)
    q_off = n_ctx // tq
    return pl.pallas_call(
        _attn_kernel,
        grid=(b, nh // 2, lq // tq, lt // tk),
        in_specs=[
            pl.BlockSpec((1, 2, tq, LANES), lambda i, h, qi, ki: (i, h, qi + q_off, 0)),
            pl.BlockSpec((1, 2, tk, LANES), lambda i, h, qi, ki: (i, h, ki, 0)),
            pl.BlockSpec((1, tk, 2 * MLA_V), lambda i, h, qi, ki: (i, ki, h)),
        ],
        out_specs=pl.BlockSpec((1, tq, 2 * MLA_V), lambda i, h, qi, ki: (i, qi, h)),
        out_shape=jax.ShapeDtypeStruct((b, lq, nh * MLA_V), F32),
        scratch_shapes=[pltpu.VMEM((2, tq, LANES), F32), pltpu.VMEM((2, tq, LANES), F32),
                        pltpu.VMEM((2, tq, LANES), F32)],
        compiler_params=_cparams(("parallel", "parallel", "parallel", "arbitrary")),
        name="mla_attention",
    )(q, k, v)


def _odd_merge_kernel(z_ref, of_ref, ob_ref, att_ref, x_ref, mod_ref, gnw_ref, wo_ref, o_ref, cat_ref):
    o = of_ref[0] + ob_ref[0]
    gate = _silu(z_ref[0])
    for hd in range(N_HEADS):
        cs = slice(hd * HEAD_DIM, (hd + 1) * HEAD_DIM)
        cat_ref[:, cs] = (_rms(o[:, cs], gnw_ref[...]) * gate[:, cs]).astype(BF16)
    cat_ref[:, MIX_HALF:] = att_ref[0].astype(BF16)
    y = jnp.dot(cat_ref[...], wo_ref[...], preferred_element_type=F32)
    o_ref[0] = x_ref[0] + mod_ref[0, 0][2:3] * y


def _odd_merge(px, o_f, o_b, att, xa, modtab, gnw, wo_bf16, nct):
    b, lt, d = xa.shape
    tm = TOKEN_TILE
    lq = att.shape[1]
    w = MIX_HALF
    lat = lambda width, blk: pl.BlockSpec((1, tm, width), lambda i, t: (i, t + nct, blk))
    const = lambda shape: pl.BlockSpec(shape, lambda i, t: (0,) * len(shape))
    return pl.pallas_call(
        _odd_merge_kernel,
        grid=(b, lq // tm),
        in_specs=[
            lat(w, 3), lat(w, 0), lat(w, 0),
            pl.BlockSpec((1, tm, w), lambda i, t: (i, t, 0)),
            lat(d, 0),
            pl.BlockSpec((1, 1, 8, d), lambda i, t: (i, 1, 0, 0)),
            const((1, HEAD_DIM)), const((2 * w, d)),
        ],
        out_specs=lat(d, 0),
        out_shape=jax.ShapeDtypeStruct((b, lt, d), F32),
        scratch_shapes=[pltpu.VMEM((tm, 2 * w), BF16)],
        input_output_aliases={4: 0},
        compiler_params=_cparams(("parallel", "parallel")),
        name="odd_merge",
    )(px, o_f, o_b, att, xa, modtab, gnw, wo_bf16)


def _router_kernel(x_ref, mod_ref, nw_ref, wr_ref, br_ref, h_ref, info_ref, cnt_ref, run_ref):
    @pl.when(jnp.logical_and(pl.program_id(0) == 0, pl.program_id(1) == 0))
    def _():
        run_ref[...] = jnp.zeros_like(run_ref)

    tm = x_ref.shape[1]
    mod = mod_ref[0, 0]
    h = _rms(x_ref[0], nw_ref[...]) * (1.0 + mod[4:5]) + mod[3:4]
    h_ref[0] = h
    logits = _dot3(h, wr_ref[...]) + br_ref[...]
    lane = lax.broadcasted_iota(jnp.int32, (tm, LANES), 1)
    lane_f = lane.astype(F32)
    big = float(LANES)
    neg = -jnp.inf
    is_g = jnp.logical_and(lane >= N_EXPERTS, lane < N_EXPERTS + N_GROUPS)
    lg = jnp.where(is_g, logits, neg)
    mg = jnp.max(lg, axis=-1, keepdims=True)
    pg_top = 1.0 / jnp.sum(jnp.exp(lg - mg), axis=-1, keepdims=True)
    g_idx = jnp.min(jnp.where(lg == mg, lane_f - N_EXPERTS, big), axis=-1, keepdims=True)
    in_g = jnp.logical_and(lane < N_EXPERTS, (lane >> 3).astype(F32) == g_idx)
    le = jnp.where(in_g, logits, neg)
    m1 = jnp.max(le, axis=-1, keepdims=True)
    e1 = jnp.min(jnp.where(le == m1, lane_f, big), axis=-1, keepdims=True)
    le2 = jnp.where(lane_f == e1, neg, le)
    m2 = jnp.max(le2, axis=-1, keepdims=True)
    e2 = jnp.min(jnp.where(le2 == m2, lane_f, big), axis=-1, keepdims=True)
    se = jnp.sum(jnp.exp(le - m1), axis=-1, keepdims=True)
    p1 = 1.0 / se
    p2 = jnp.exp(m2 - m1) / se
    w1 = pg_top * p1 / (p1 + p2)
    w2 = pg_top * p2 / (p1 + p2)
    hit1 = lane_f == e1
    hit2 = lane_f == e2
    oh = jnp.where(jnp.logical_or(hit1, hit2), 1.0, 0.0)
    r = lax.broadcasted_iota(jnp.int32, (tm, tm), 0)
    c = lax.broadcasted_iota(jnp.int32, (tm, tm), 1)
    before = jnp.dot(jnp.where(c < r, 1.0, 0.0).astype(BF16), oh.astype(BF16), preferred_element_type=F32)
    base = before + run_ref[0:1, :]
    pos1 = jnp.sum(jnp.where(hit1, base, 0.0), axis=-1, keepdims=True)
    pos2 = jnp.sum(jnp.where(hit2, base, 0.0), axis=-1, keepdims=True)
    run = run_ref[0:1, :] + jnp.sum(oh, axis=0, keepdims=True)
    run_ref[...] = jnp.broadcast_to(run, run_ref.shape)
    cnt_ref[...] = jnp.broadcast_to(run, cnt_ref.shape)
    info = jnp.zeros((tm, LANES), F32)
    for j, val in enumerate((e1, e2, pos1, pos2, w1, w2)):
        info = jnp.where(lane == j, val, info)
    info_ref[0] = info


def _router(xa, modtab, norm_w, wr, br, nct):
    b, lt, d = xa.shape
    tm = TOKEN_TILE
    const = lambda shape: pl.BlockSpec(shape, lambda i, t: (0,) * len(shape))
    return pl.pallas_call(
        _router_kernel,
        grid=(b, lt // tm),
        in_specs=[
            pl.BlockSpec((1, tm, d), lambda i, t: (i, t, 0)),
            pl.BlockSpec((1, 1, 8, d), _mod_index(nct)),
            const((1, d)), const((d, LANES)), const((1, LANES)),
        ],
        out_specs=[
            pl.BlockSpec((1, tm, d), lambda i, t: (i, t, 0)),
            pl.BlockSpec((1, tm, LANES), lambda i, t: (i, t, 0)),
            const((SUBLANES, LANES)),
        ],
        out_shape=[
            jax.ShapeDtypeStruct((b, lt, d), F32),
            jax.ShapeDtypeStruct((b, lt, LANES), F32),
            jax.ShapeDtypeStruct((SUBLANES, LANES), F32),
        ],
        scratch_shapes=[pltpu.VMEM((SUBLANES, LANES), F32)],
        compiler_params=_cparams(("arbitrary", "arbitrary")),
        name="moe_router",
    )(xa, modtab, norm_w.reshape(1, d), wr, br)


def _dispatch_kernel(dest_ref, h_ref, xs_in_ref, xs_ref, sem):
    del xs_in_ref
    tm = h_ref.shape[0]

    def row_copy(r, k):
        return pltpu.make_async_copy(h_ref.at[pl.ds(r, 1)], xs_ref.at[pl.ds(dest_ref[0, 0, 2 * r + k], 1)], sem)

    def issue(r, carry):
        row_copy(r, 0).start()
        row_copy(r, 1).start()
        return carry

    def drain(r, carry):
        row_copy(r, 0).wait()
        row_copy(r, 1).wait()
        return carry

    lax.fori_loop(0, tm, issue, 0)
    lax.fori_loop(0, tm, drain, 0)


def _dispatch(dest3, h2d, n_slots):
    t, d = h2d.shape
    tm = TOKEN_TILE
    xs0 = jnp.zeros((n_slots, d), F32)
    return pl.pallas_call(
        _dispatch_kernel,
        grid=(t // tm,),
        in_specs=[
            pl.BlockSpec((1, 1, 2 * tm), lambda i: (i, 0, 0), memory_space=pltpu.SMEM),
            pl.BlockSpec((tm, d), lambda i: (i, 0)),
            pl.BlockSpec(memory_space=pl.ANY),
        ],
        out_specs=pl.BlockSpec(memory_space=pl.ANY),
        out_shape=jax.ShapeDtypeStruct((n_slots, d), F32),
        scratch_shapes=[pltpu.SemaphoreType.DMA],
        input_output_aliases={2: 0},
        compiler_params=_cparams(("arbitrary",)),
        name="moe_dispatch",
    )(dest3, h2d, xs0)


def _expert_kernel(be_ref, nu_ref, xs_ref, wg_ref, wu_ref, wd_ref, ys_ref):
    i = pl.program_id(0)

    @pl.when(i < nu_ref[0])
    def _():
        x = xs_ref[...].astype(BF16)
        a = jnp.dot(x, wg_ref[0], preferred_element_type=F32)
        u = jnp.dot(x, wu_ref[0], preferred_element_type=F32)
        ys_ref[...] = jnp.dot((_silu(a) * u).astype(BF16), wd_ref[0], preferred_element_type=F32)

    @pl.when(i >= nu_ref[0])
    def _():
        ys_ref[...] = jnp.zeros_like(ys_ref)


def _experts(blk_e, n_used, xs, wg, wu, wd):
    n_slots, d = xs.shape
    ff = wg.shape[2]
    nb = n_slots // MOE_BLOCK
    grid_spec = pltpu.PrefetchScalarGridSpec(
        num_scalar_prefetch=2,
        grid=(nb,),
        in_specs=[
            pl.BlockSpec((MOE_BLOCK, d), lambda i, be, nu: (jnp.minimum(i, jnp.maximum(nu[0] - 1, 0)), 0)),
            pl.BlockSpec((1, d, ff), lambda i, be, nu: (be[i], 0, 0)),
            pl.BlockSpec((1, d, ff), lambda i, be, nu: (be[i], 0, 0)),
            pl.BlockSpec((1, ff, d), lambda i, be, nu: (be[i], 0, 0)),
        ],
        out_specs=pl.BlockSpec((MOE_BLOCK, d), lambda i, be, nu: (i, 0)),
    )
    return pl.pallas_call(
        _expert_kernel,
        grid_spec=grid_spec,
        out_shape=jax.ShapeDtypeStruct((n_slots, d), F32),
        compiler_params=_cparams(("arbitrary",)),
        name="moe_experts",
    )(blk_e, n_used, xs, wg, wu, wd)


def _combine_kernel(dest_ref, info_ref, x_ref, mod_ref, ys_ref, o_ref, buf_ref, sem):
    tm = x_ref.shape[1]

    def row_copy(r, k):
        return pltpu.make_async_copy(ys_ref.at[pl.ds(dest_ref[0, 0, 2 * r + k], 1)],
                                     buf_ref.at[k, pl.ds(r, 1)], sem)

    def issue(r, carry):
        row_copy(r, 0).start()
        row_copy(r, 1).start()
        return carry

    def drain(r, carry):
        row_copy(r, 0).wait()
        row_copy(r, 1).wait()
        return carry

    lax.fori_loop(0, tm, issue, 0)
    lax.fori_loop(0, tm, drain, 0)
    info = info_ref[0]
    y = info[:, 4:5] * buf_ref[0] + info[:, 5:6] * buf_ref[1]
    o_ref[0] = x_ref[0] + mod_ref[0, 0][5:6] * y


def _combine(dest3, info, xa, modtab, ys, nct):
    b, lt, d = xa.shape
    tm = TOKEN_TILE
    nt = lt // tm
    return pl.pallas_call(
        _combine_kernel,
        grid=(b, nt),
        in_specs=[
            pl.BlockSpec((1, 1, 2 * tm), lambda i, t: (i * nt + t, 0, 0), memory_space=pltpu.SMEM),
            pl.BlockSpec((1, tm, LANES), lambda i, t: (i, t, 0)),
            pl.BlockSpec((1, tm, d), lambda i, t: (i, t, 0)),
            pl.BlockSpec((1, 1, 8, d), _mod_index(nct)),
            pl.BlockSpec(memory_space=pl.ANY),
        ],
        out_specs=pl.BlockSpec((1, tm, d), lambda i, t: (i, t, 0)),
        out_shape=jax.ShapeDtypeStruct((b, lt, d), F32),
        scratch_shapes=[pltpu.VMEM((2, tm, d), F32), pltpu.SemaphoreType.DMA],
        input_output_aliases={2: 0},
        compiler_params=_cparams(("arbitrary", "arbitrary")),
        name="moe_combine",
    )(dest3, info, xa, modtab, ys)


def _moe(xa, modtab, norm_w, wg_r, bg_r, we_r, be_r, w_gate, w_up, w_down, nct):
    b, lt, d = xa.shape
    t = b * lt
    tm = TOKEN_TILE
    pad = LANES - N_EXPERTS - N_GROUPS
    wr = jnp.concatenate([we_r, wg_r, jnp.zeros((d, pad), F32)], axis=1)
    br = jnp.concatenate([be_r, bg_r, jnp.zeros((pad,), F32)]).reshape(1, LANES)
    h, info, cnt = _router(xa, modtab, norm_w, wr, br, nct)
    counts = cnt[0, :N_EXPERTS].astype(jnp.int32)
    padded = (counts + MOE_BLOCK - 1) // MOE_BLOCK * MOE_BLOCK
    pend = jnp.cumsum(padded)
    pstart = pend - padded
    info2 = info.reshape(t, LANES)
    e_idx = info2[:, 0:2].astype(jnp.int32)
    dest = pstart[e_idx] + info2[:, 2:4].astype(jnp.int32)
    dest3 = dest.reshape(t // tm, 1, 2 * tm)
    n_blk = -(-2 * t // MOE_BLOCK) + N_EXPERTS
    blk_e = jnp.minimum(jnp.searchsorted(pend, jnp.arange(n_blk, dtype=jnp.int32) * MOE_BLOCK, side='right'),
                        N_EXPERTS - 1).astype(jnp.int32)
    n_used = (pend[-1:] // MOE_BLOCK).astype(jnp.int32)
    xs = _dispatch(dest3, h.reshape(t, d), n_blk * MOE_BLOCK)
    ys = _experts(blk_e, n_used, xs, w_gate.astype(BF16), w_up.astype(BF16), w_down.astype(BF16))
    return _combine(dest3, info, xa, modtab, ys, nct)


_ROPE_SWAP = np.concatenate([np.arange(8, 16), np.arange(0, 8), np.arange(24, 32), np.arange(16, 24)])


def _odd_in_weights(w_in):
    d = w_in.shape[0]
    w4 = 4 * MIX_HALF
    o_beta = w4
    o_cq = o_beta + 4 * N_HEADS
    o_ckv = o_cq + MLA_Q_RANK
    o_kr = o_ckv + MLA_KV_RANK
    z = lambda n: jnp.zeros((d, n), F32)
    kr = w_in[:, o_kr:o_kr + MLA_ROPE]
    blocks = [
        w_in[:, :w4], w_in[:, o_cq:o_ckv], w_in[:, o_ckv:o_kr],
        z(MLA_NOPE), kr, z(LANES - MLA_QK),
        z(MLA_NOPE), kr[:, _ROPE_SWAP], z(LANES - MLA_QK),
        w_in[:, o_beta:o_cq], z(LANES - 4 * N_HEADS),
    ]
    return jnp.concatenate(blocks, axis=1)


def _per_head_lanes(w, per, take, width):
    r = w.shape[0]
    wh = w.reshape(r, -1, per)[:, :, take]
    return jnp.pad(wh, ((0, 0), (0, 0), (0, width - wh.shape[2]))).reshape(r, -1)


def _rope_lane_weights(w):
    plain = jnp.pad(w, (0, LANES - MLA_QK))
    sw = jnp.pad(jnp.concatenate([jnp.zeros((MLA_NOPE,), F32), w[MLA_NOPE:][_ROPE_SWAP]]), (0, LANES - MLA_QK))
    return jnp.pad(jnp.stack([plain, sw]), ((0, 6), (0, 0)))


def _rope_tables(n_ctx, n_lat):
    quarter = MLA_ROPE // 4
    freqs = ROPE_THETA ** (-jnp.arange(quarter, dtype=F32) / quarter)
    pos = jnp.arange(n_lat)
    row = (pos // GRID_W).astype(F32)[:, None] * freqs
    col = (pos % GRID_W).astype(F32)[:, None] * freqs
    ones = jnp.ones((n_lat, MLA_NOPE), F32)
    zpad = jnp.zeros((n_lat, LANES - MLA_QK), F32)
    ctab = jnp.concatenate([ones, jnp.cos(row), jnp.cos(row), jnp.cos(col), jnp.cos(col), zpad], axis=1)
    stab = jnp.concatenate([0 * ones, -jnp.sin(row), jnp.sin(row), -jnp.sin(col), jnp.sin(col), zpad], axis=1)
    c_ctx = jnp.concatenate([jnp.ones((n_ctx, MLA_QK), F32), jnp.zeros((n_ctx, LANES - MLA_QK), F32)], axis=1)
    return jnp.concatenate([c_ctx, ctab], axis=0), jnp.concatenate([jnp.zeros((n_ctx, LANES), F32), stab], axis=0)


def _lower_bound(logits, layer):
    return jnp.cumsum(jax.nn.softmax(logits.astype(F32), axis=0), axis=0)[layer]


def kernel(x, c, ctx, c_ctx, ada_w, ada_b, norm_mix_w, norm_ffn_w, even_w_in, even_w_out, gmlp_norm_w, gmlp_ws, gmlp_bs, hgrn_lb_logits, hgrn_norm_w, odd_w_in, odd_w_out, gdn_conv_w, gdn_a_log, gdn_dt_bias, gdn_norm_w, mla_q_norm_w, mla_wq_up, mla_kv_norm_w, mla_wkv_up, mla_qk_norm_q, mla_qk_norm_k, router_group_w, router_group_b, router_expert_w, router_expert_b, moe_w_gate, moe_w_up, moe_w_down):
    b, n_lat, d = x.shape
    n_ctx = ctx.shape[1]
    depth = ada_w.shape[0]
    tm = TOKEN_TILE
    assert depth == 2 and n_ctx % tm == 0 and n_lat % tm == 0 and n_lat % GRID_W == 0
    nct = n_ctx // tm
    ncc = n_ctx // CHUNK

    xa = jnp.concatenate([ctx, x], axis=1)
    rows = -(-(b + 1) // SUBLANES) * SUBLANES
    cvec = jnp.zeros((rows, d), F32).at[:b].set(c).at[b].set(c_ctx)
    mods = _mod_vectors(cvec, ada_w, ada_b).reshape(depth, rows, 6, d)

    def mod_table(layer):
        mx = mods[layer, :b]
        mc = jnp.broadcast_to(mods[layer, b], (b, 6, d))
        return jnp.pad(jnp.stack([mc, mx], axis=1), ((0, 0), (0, 0), (0, 2), (0, 0)))

    modtab = mod_table(0)
    px = _in_projection(xa, modtab, norm_mix_w[0], even_w_in[0].astype(BF16), nct)
    lb_f = _lower_bound(hgrn_lb_logits[0], 0).reshape(1, MIX_HALF)
    lb_b = _lower_bound(hgrn_lb_logits[1], 0).reshape(1, MIX_HALF)
    o_f = _hgrn_scan(px, lb_f, ncc, False, 4)
    o_b = _hgrn_scan(px, lb_b, ncc, True, 5)
    bsb = jnp.broadcast_to(gmlp_bs[0][:, :, None], (N_HEADS, CHUNK, HEAD_DIM))
    xa = _even_merge(px, o_f, o_b, xa, modtab, gmlp_norm_w[0].reshape(1, MIX_HALF), gmlp_ws[0].astype(BF16), bsb,
                     hgrn_norm_w[0].reshape(1, MIX_HALF), even_w_out[0].astype(BF16), nct)
    xa = _moe(xa, modtab, norm_ffn_w[0], router_group_w[0], router_group_b[0], router_expert_w[0],
              router_expert_b[0], moe_w_gate[0], moe_w_up[0], moe_w_down[0], nct)

    modtab = mod_table(1)
    px = _in_projection(xa, modtab, norm_mix_w[1], _odd_in_weights(odd_w_in[0]).astype(BF16), nct)
    cw = jnp.pad(gdn_conv_w[0], ((0, 8 - GDN_CONV), (0, 0)))
    gvec = jnp.zeros((8, LANES), F32)
    gvec = gvec.at[0, 2 * N_HEADS:4 * N_HEADS].set(gdn_a_log[0].reshape(-1))
    gvec = gvec.at[1, 2 * N_HEADS:4 * N_HEADS].set(gdn_dt_bias[0].reshape(-1))
    wq = mla_wq_up[0]
    rope_cols = MLA_NOPE + _ROPE_SWAP
    wq_all = jnp.concatenate([
        _per_head_lanes(wq, MLA_QK, np.arange(MLA_QK), LANES),
        jnp.pad(_per_head_lanes(wq, MLA_QK, rope_cols, LANES - MLA_NOPE).reshape(MLA_Q_RANK, MLA_HEADS, -1),
                ((0, 0), (0, 0), (MLA_NOPE, 0)))[:, :, :LANES].reshape(MLA_Q_RANK, -1),
    ], axis=1).astype(BF16)
    wkv = mla_wkv_up[0]
    wkv_all = jnp.concatenate([
        _per_head_lanes(wkv, MLA_NOPE + MLA_V, np.arange(MLA_NOPE), LANES),
        _per_head_lanes(wkv, MLA_NOPE + MLA_V, MLA_NOPE + np.arange(MLA_V), MLA_V),
    ], axis=1).astype(BF16)
    ctab, stab = _rope_tables(n_ctx, n_lat)
    qkv, gates, q, k, v = _odd_prep(px, cw, gvec, mla_q_norm_w[0].reshape(1, -1), wq_all,
                                    _rope_lane_weights(mla_qk_norm_q[0]), mla_kv_norm_w[0].reshape(1, -1), wkv_all,
                                    _rope_lane_weights(mla_qk_norm_k[0]), ctab, stab, nct)
    o_f = _gdn_scan(qkv, gates, ncc, False)
    o_b = _gdn_scan(qkv, gates, ncc, True)
    att = _attention(q, k, v, n_ctx)
    xa = _odd_merge(px, o_f, o_b, att, xa, modtab, gdn_norm_w[0].reshape(1, HEAD_DIM), odd_w_out[0].astype(BF16), nct)
    xa = _moe(xa, modtab, norm_ffn_w[1], router_group_w[1], router_group_b[1], router_expert_w[1],
              router_expert_b[1], moe_w_gate[1], moe_w_up[1], moe_w_down[1], nct)
    return xa[:, n_ctx:]
```

```python
import functools
import math

import jax
import jax.numpy as jnp
import numpy as np
from jax import lax
from jax.experimental import pallas as pl
from jax.experimental.pallas import tpu as pltpu

F32 = jnp.float32
BF16 = jnp.bfloat16
EPS = 1e-6

LANES = 128
SUBLANES = 8
TOKEN_TILE = 256
CHUNK = 128
VMEM_LIMIT = 56 * 1024 * 1024

GRID_W = 64
ROPE_THETA = 10000.0
HEAD_DIM = 128
N_HEADS = 4
MIX_HALF = N_HEADS * HEAD_DIM
GDN_CONV = 5
MLA_HEADS = 8
MLA_NOPE = 64
MLA_ROPE = 32
MLA_V = 64
MLA_QK = MLA_NOPE + MLA_ROPE
MLA_Q_RANK = 256
MLA_KV_RANK = 128
N_GROUPS = 4
EXPERTS_PER_GROUP = 8
N_EXPERTS = N_GROUPS * EXPERTS_PER_GROUP
MOE_BLOCK = 256
ROW_DMA_UNROLL = 8
ATTN_TQ_CANDIDATES = (512, 256)
ATTN_TK_CANDIDATES = (768, 512, 384, 256, 128)

NT_DIMS = (((1,), (1,)), ((), ()))
TN_DIMS = (((0,), (0,)), ((), ()))


def _cparams(sem):
    return pltpu.CompilerParams(dimension_semantics=sem, vmem_limit_bytes=VMEM_LIMIT)


def _sigmoid(x):
    return 1.0 / (1.0 + jnp.exp(-x))


def _silu(x):
    return x * _sigmoid(x)


def _gelu_tanh(x):
    c = math.sqrt(2.0 / math.pi)
    return 0.5 * x * (1.0 + jnp.tanh(c * (x + 0.044715 * (x * x * x))))


def _softplus(x):
    return jnp.maximum(x, 0.0) + jnp.log(1.0 + jnp.exp(-jnp.abs(x)))


def _bdot(a, b):
    return jnp.dot(a.astype(BF16), b.astype(BF16), preferred_element_type=F32)


def _bdot_nt(a, b):
    return lax.dot_general(a.astype(BF16), b.astype(BF16), NT_DIMS, preferred_element_type=F32)


def _bdot_tn(a, b):
    return lax.dot_general(a.astype(BF16), b.astype(BF16), TN_DIMS, preferred_element_type=F32)


BMM_DIMS = (((2,), (1,)), ((0,), (0,)))
BMM_NT_DIMS = (((2,), (2,)), ((0,), (0,)))
BMM_TN_DIMS = (((1,), (1,)), ((0,), (0,)))


def _bmm(a, b, dims=BMM_DIMS):
    return lax.dot_general(a.astype(BF16), b.astype(BF16), dims, preferred_element_type=F32)


def _split_bf16(x, n):
    parts = []
    r = x
    for _ in range(n):
        p = r.astype(BF16)
        parts.append(p)
        r = r - p.astype(F32)
    return parts


def _dot_exact_lhs(m01, x):
    acc = None
    for p in _split_bf16(x, 3):
        t = jnp.dot(m01, p, preferred_element_type=F32)
        acc = t if acc is None else acc + t
    return acc


def _dot3(a, b):
    a_hi, a_lo = _split_bf16(a, 2)
    b_hi, b_lo = _split_bf16(b, 2)
    d = lambda u, v: jnp.dot(u, v, preferred_element_type=F32)
    return d(a_hi, b_hi) + (d(a_hi, b_lo) + d(a_lo, b_hi))


def _rms(x, w, n=None):
    n = x.shape[-1] if n is None else n
    ms = jnp.sum(x * x, axis=-1, keepdims=True) * (1.0 / n)
    return x * lax.rsqrt(ms + EPS) * w


def _mod_kernel(c_ref, w_ref, b_ref, o_ref):
    s = _silu(c_ref[...])
    o_ref[0] = _dot3(s, w_ref[0]) + b_ref[0]


def _mod_vectors(cvec, ada_w, ada_b):
    depth, d, n = ada_w.shape
    rows = cvec.shape[0]
    tn = 512
    return pl.pallas_call(
        _mod_kernel,
        grid=(depth, n // tn),
        in_specs=[
            pl.BlockSpec((rows, d), lambda l, j: (0, 0)),
            pl.BlockSpec((1, d, tn), lambda l, j: (l, 0, j)),
            pl.BlockSpec((1, 1, tn), lambda l, j: (l, 0, j)),
        ],
        out_specs=pl.BlockSpec((1, rows, tn), lambda l, j: (l, 0, j)),
        out_shape=jax.ShapeDtypeStruct((depth, rows, n), F32),
        compiler_params=_cparams(("parallel", "parallel")),
        name="adaln_mod",
    )(cvec, ada_w, ada_b.reshape(depth, 1, n))


def _mod_index(nct):
    return lambda b, t: (b, jnp.where(t < nct, 0, 1), 0, 0)


def _inproj_kernel(x_ref, mod_ref, nw_ref, w_ref, o_ref, *, shift_row, tn):
    mod = mod_ref[0, 0]
    h = _rms(x_ref[0], nw_ref[...]) * (1.0 + mod[shift_row + 1:shift_row + 2]) + mod[shift_row:shift_row + 1]
    hb = h.astype(BF16)
    for j in range(w_ref.shape[1] // tn):
        o_ref[0, :, j * tn:(j + 1) * tn] = jnp.dot(hb, w_ref[:, j * tn:(j + 1) * tn], preferred_element_type=F32)


def _in_projection(xa, modtab, norm_w, w_bf16, nct):
    b, lt, d = xa.shape
    n = w_bf16.shape[1]
    tm = TOKEN_TILE
    tn = 512 if n % 512 == 0 else 256
    return pl.pallas_call(
        functools.partial(_inproj_kernel, shift_row=0, tn=tn),
        grid=(b, lt // tm),
        in_specs=[
            pl.BlockSpec((1, tm, d), lambda i, t: (i, t, 0)),
            pl.BlockSpec((1, 1, 8, d), _mod_index(nct)),
            pl.BlockSpec((1, d), lambda i, t: (0, 0)),
            pl.BlockSpec((d, n), lambda i, t: (0, 0)),
        ],
        out_specs=pl.BlockSpec((1, tm, n), lambda i, t: (i, t, 0)),
        out_shape=jax.ShapeDtypeStruct((b, lt, n), F32),
        compiler_params=_cparams(("parallel", "parallel")),
        name="in_projection",
    )(xa, modtab, norm_w.reshape(1, d), w_bf16)


def _chunk_order(nc_ctx, nc_all, rev):
    if not rev:
        return lambda s: s
    return lambda s: jnp.where(s < nc_ctx, nc_ctx - 1 - s, nc_all - 1 - (s - nc_ctx))


def _tri01(n, rev):
    r = lax.broadcasted_iota(jnp.int32, (n, n), 0)
    c = lax.broadcasted_iota(jnp.int32, (n, n), 1)
    keep = (c >= r) if rev else (c <= r)
    return jnp.where(keep, 1.0, 0.0).astype(BF16)


def _level_ref_rows(g_ref, lo, hi, half, rev):
    seg = 2 * half
    nseg = CHUNK // seg
    nb = hi - lo
    row_of = lambda s: s * seg + half - (0 if rev else 1)
    if seg >= SUBLANES:
        pieces = [jnp.broadcast_to(g_ref[lo:hi, pl.ds(row_of(s), 1), :], (nb, seg, LANES)) for s in range(nseg)]
        return jnp.concatenate(pieces, axis=1)
    sub = lax.broadcasted_iota(jnp.int32, (nb, SUBLANES, LANES), 1) >> int(math.log2(seg))
    per = SUBLANES // seg
    pieces = []
    for v in range(CHUNK // SUBLANES):
        acc = jnp.broadcast_to(g_ref[lo:hi, pl.ds(row_of(v * per), 1), :], (nb, SUBLANES, LANES))
        for j in range(1, per):
            cand = jnp.broadcast_to(g_ref[lo:hi, pl.ds(row_of(v * per + j), 1), :], (nb, SUBLANES, LANES))
            acc = jnp.where(sub >= j, cand, acc)
        pieces.append(acc)
    return jnp.concatenate(pieces, axis=1)


def _hgrn_kernel(qf_ref, vf_ref, ff_ref, qb_ref, vb_ref, fb_ref, lbf_ref, lbb_ref, of_ref, ob_ref, st_ref, g_ref):
    @pl.when(pl.program_id(1) == 0)
    def _():
        st_ref[...] = jnp.zeros_like(st_ref)

    c = CHUNK
    nb = 2 * N_HEADS
    heads = lambda ref, lead: [ref[lead, :, hd * HEAD_DIM:(hd + 1) * HEAD_DIM] for hd in range(N_HEADS)]
    q = jnp.stack(heads(qf_ref, 0) + heads(qb_ref, 0))
    v = jnp.stack(heads(vf_ref, 0) + heads(vb_ref, 0))
    fr = jnp.stack(heads(ff_ref, 0) + heads(fb_ref, 0))
    lb = jnp.stack([lbf_ref[:, hd * HEAD_DIM:(hd + 1) * HEAD_DIM] for hd in range(N_HEADS)]
                   + [lbb_ref[:, hd * HEAD_DIM:(hd + 1) * HEAD_DIM] for hd in range(N_HEADS)])
    f = lb + (1.0 - lb) * _sigmoid(fr)
    k = 1.0 - f

    bi = lax.broadcasted_iota(jnp.int32, (nb, c, c), 0)
    ri = lax.broadcasted_iota(jnp.int32, (nb, c, c), 1)
    ci = lax.broadcasted_iota(jnp.int32, (nb, c, c), 2)
    rev = bi >= N_HEADS
    incl = jnp.logical_or(jnp.logical_and(rev, ci >= ri), jnp.logical_and(jnp.logical_not(rev), ci <= ri))
    tri = jnp.where(incl, 1.0, 0.0).astype(BF16)
    g = None
    for part in _split_bf16(jnp.log(f), 3):
        t = lax.dot_general(tri, part, BMM_DIMS, preferred_element_type=F32)
        g = t if g is None else g + t
    g_ref[...] = g

    rows = lax.broadcasted_iota(jnp.int32, (nb, c, LANES), 1)
    rev_rows = lax.broadcasted_iota(jnp.int32, (nb, c, LANES), 0) >= N_HEADS
    a = jnp.where(ri == ci, jnp.sum(q * k, axis=-1, keepdims=True), 0.0)
    for lv in range(int(math.log2(c))):
        half = 1 << lv
        r = jnp.concatenate([_level_ref_rows(g_ref, 0, N_HEADS, half, False),
                             _level_ref_rows(g_ref, N_HEADS, nb, half, True)], axis=0)
        e = jnp.exp(-jnp.abs(g - r))
        is_q = jnp.logical_xor((rows & half) != 0, rev_rows)
        qe = jnp.where(is_q, q * e, 0.0)
        ke = jnp.where(is_q, 0.0, k * e)
        same = (ri >> (lv + 1)) == (ci >> (lv + 1))
        a = a + jnp.where(same, _bmm(qe, ke, BMM_NT_DIMS), 0.0)
    g_tot = jnp.concatenate([g_ref[0:N_HEADS, pl.ds(c - 1, 1), :], g_ref[N_HEADS:nb, pl.ds(0, 1), :]], axis=0)
    st = st_ref[...]
    o = _bmm(a, v) + _bmm(q * jnp.exp(g), st, BMM_NT_DIMS)
    st_ref[...] = st * jnp.exp(g_tot) + _bmm(v, k * jnp.exp(g_tot - g), BMM_TN_DIMS)
    for hd in range(N_HEADS):
        of_ref[0, :, hd * HEAD_DIM:(hd + 1) * HEAD_DIM] = o[hd]
        ob_ref[0, :, hd * HEAD_DIM:(hd + 1) * HEAD_DIM] = o[N_HEADS + hd]


def _hgrn_scan(px, lb_f, lb_b, nc_ctx):
    b, lt, _ = px.shape
    nc = lt // CHUNK
    w = MIX_HALF
    order_b = _chunk_order(nc_ctx, nc, True)
    fwd = lambda blk: pl.BlockSpec((1, CHUNK, w), lambda i, s: (i, s, blk))
    bwd = lambda blk: pl.BlockSpec((1, CHUNK, w), lambda i, s: (i, order_b(s), blk))
    const = pl.BlockSpec((1, w), lambda i, s: (0, 0))
    return pl.pallas_call(
        _hgrn_kernel,
        grid=(b, nc),
        in_specs=[fwd(2), fwd(3), fwd(4), bwd(2), bwd(3), bwd(5), const, const],
        out_specs=[fwd(0), bwd(0)],
        out_shape=[jax.ShapeDtypeStruct((b, lt, w), F32), jax.ShapeDtypeStruct((b, lt, w), F32)],
        scratch_shapes=[pltpu.VMEM((2 * N_HEADS, HEAD_DIM, HEAD_DIM), F32),
                        pltpu.VMEM((2 * N_HEADS, CHUNK, HEAD_DIM), F32)],
        compiler_params=_cparams(("parallel", "arbitrary")),
        name="hgrn_scan",
    )(px, px, px, px, px, px, lb_f, lb_b)


def _even_merge_kernel(pa_ref, pg_ref, of_ref, ob_ref, x_ref, mod_ref, gnw_ref, ws_ref, bsb_ref,
                       hnw_ref, wo_ref, o_ref, cat_ref):
    tm = pa_ref.shape[1]
    z = _gelu_tanh(pa_ref[0])
    u = z[:, :MIX_HALF]
    v = _rms(z[:, MIX_HALF:], gnw_ref[...])
    for cc in range(tm // CHUNK):
        rs = slice(cc * CHUNK, (cc + 1) * CHUNK)
        for g in range(N_HEADS):
            cs = slice(g * HEAD_DIM, (g + 1) * HEAD_DIM)
            s = _bdot(ws_ref[g], v[rs, cs]) + bsb_ref[g]
            cat_ref[rs, cs] = (u[rs, cs] * s).astype(BF16)
    o = of_ref[0] + ob_ref[0]
    gate = _silu(pg_ref[0])
    for hd in range(N_HEADS):
        cs = slice(hd * HEAD_DIM, (hd + 1) * HEAD_DIM)
        rec = _rms(o[:, cs], hnw_ref[:, cs]) * gate[:, cs]
        cat_ref[:, MIX_HALF + hd * HEAD_DIM:MIX_HALF + (hd + 1) * HEAD_DIM] = rec.astype(BF16)
    y = jnp.dot(cat_ref[...], wo_ref[...], preferred_element_type=F32)
    o_ref[0] = x_ref[0] + mod_ref[0, 0][2:3] * y


def _even_merge(px, o_f, o_b, xa, modtab, gnw, ws_bf16, bsb, hnw, wo_bf16, nct):
    b, lt, d = xa.shape
    tm = TOKEN_TILE
    w = MIX_HALF
    tile = lambda width, blk: pl.BlockSpec((1, tm, width), lambda i, t: (i, t, blk))
    const = lambda shape: pl.BlockSpec(shape, lambda i, t: (0,) * len(shape))
    return pl.pallas_call(
        _even_merge_kernel,
        grid=(b, lt // tm),
        in_specs=[
            tile(2 * w, 0), tile(w, 6), tile(w, 0), tile(w, 0), tile(d, 0),
            pl.BlockSpec((1, 1, 8, d), _mod_index(nct)),
            const((1, w)), const((N_HEADS, CHUNK, CHUNK)), const((N_HEADS, CHUNK, HEAD_DIM)),
            const((1, w)), const((2 * w, d)),
        ],
        out_specs=tile(d, 0),
        out_shape=jax.ShapeDtypeStruct((b, lt, d), F32),
        scratch_shapes=[pltpu.VMEM((tm, 2 * w), BF16)],
        input_output_aliases={4: 0},
        compiler_params=_cparams(("parallel", "parallel")),
        name="even_merge",
    )(px, px, o_f, o_b, xa, modtab, gnw, ws_bf16, bsb, hnw, wo_bf16)


def _odd_prep_kernel(p_ref, prev_ref, next_ref, cw_ref, gvec_ref, qnw_ref, wq_ref, qkw_ref, kvnw_ref,
                     wkv_ref, kkw_ref, ct_ref, st_ref,
                     qkv_ref, gates_ref, q_ref, k_ref, v_ref, *, nct, ntiles, scale):
    tm = p_ref.shape[1]
    t = pl.program_id(1)
    w3 = 3 * MIX_HALF
    first = jnp.logical_or(t == 0, t == nct)
    last = jnp.logical_or(t == nct - 1, t == ntiles - 1)
    xin = p_ref[0, :, :w3]
    prev8 = jnp.where(first, 0.0, prev_ref[0])
    next8 = jnp.where(last, 0.0, next_ref[0])
    row8 = lax.broadcasted_iota(jnp.int32, (SUBLANES, w3), 0)
    pad = (GDN_CONV - 1) // 2
    acc = xin * cw_ref[pad:pad + 1, :]
    for sh in range(1, pad + 1):
        dn = pltpu.roll(xin, sh, 0)
        top = jnp.where(row8 < sh, pltpu.roll(prev8, sh, 0), dn[:SUBLANES])
        dn = jnp.concatenate([top, dn[SUBLANES:]], axis=0)
        acc = acc + dn * cw_ref[pad - sh:pad - sh + 1, :]
        up = pltpu.roll(xin, tm - sh, 0)
        bot = jnp.where(row8 >= SUBLANES - sh, pltpu.roll(next8, SUBLANES - sh, 0), up[tm - SUBLANES:])
        up = jnp.concatenate([up[:tm - SUBLANES], bot], axis=0)
        acc = acc + up * cw_ref[pad + sh:pad + sh + 1, :]
    act = _silu(acc)
    for hd in range(N_HEADS):
        for part, mul in ((0, HEAD_DIM ** -0.5), (1, 1.0)):
            cs = slice(part * MIX_HALF + hd * HEAD_DIM, part * MIX_HALF + (hd + 1) * HEAD_DIM)
            a = act[:, cs]
            qkv_ref[0, :, cs] = a * lax.rsqrt(jnp.sum(a * a, axis=-1, keepdims=True) + EPS) * mul
    qkv_ref[0, :, 2 * MIX_HALF:] = act[:, 2 * MIX_HALF:]

    gx = p_ref[0, :, 21 * LANES:22 * LANES]
    lane = lax.broadcasted_iota(jnp.int32, (tm, LANES), 1)
    a_neg = -jnp.exp(gvec_ref[0:1, :])
    gates_ref[0] = jnp.where(lane < 2 * N_HEADS, _sigmoid(gx), a_neg * _softplus(gx + gvec_ref[1:2, :]))

    ct = ct_ref[...]
    st = st_ref[...]
    inv_n = 1.0 / MLA_QK
    cq = _rms(p_ref[0, :, 16 * LANES:18 * LANES], qnw_ref[...])
    qraw = jnp.dot(cq.astype(BF16), wq_ref[...], preferred_element_type=F32)
    hw = MLA_HEADS * LANES
    for hd in range(MLA_HEADS):
        qa = qraw[:, hd * LANES:(hd + 1) * LANES]
        qs = qraw[:, hw + hd * LANES:hw + (hd + 1) * LANES]
        rs = lax.rsqrt(jnp.sum(qa * qa, axis=-1, keepdims=True) * inv_n + EPS)
        rot = (qa * rs * qkw_ref[0:1, :]) * ct + (qs * rs * qkw_ref[1:2, :]) * st
        q_ref[0, hd] = (rot * scale).astype(BF16)
    ckv = _rms(p_ref[0, :, 18 * LANES:19 * LANES], kvnw_ref[...])
    kv = jnp.dot(ckv.astype(BF16), wkv_ref[...], preferred_element_type=F32)
    kr = p_ref[0, :, 19 * LANES:20 * LANES]
    krs = p_ref[0, :, 20 * LANES:21 * LANES]
    for hd in range(MLA_HEADS):
        ka = kv[:, hd * LANES:(hd + 1) * LANES] + kr
        rs = lax.rsqrt(jnp.sum(ka * ka, axis=-1, keepdims=True) * inv_n + EPS)
        rot = (ka * rs * kkw_ref[0:1, :]) * ct + (krs * rs * kkw_ref[1:2, :]) * st
        k_ref[0, hd] = rot.astype(BF16)
        v_ref[0, hd] = jnp.where(lane < MLA_V, kv[:, hw + hd * LANES:hw + (hd + 1) * LANES], 1.0).astype(BF16)


def _odd_prep(px, cw, gvec, qnw, wq, qkw, kvnw, wkv, kkw, ctab, stab, nct):
    b, lt, n = px.shape
    tm = TOKEN_TILE
    ntiles = lt // tm
    w3 = 3 * MIX_HALF
    per = tm // SUBLANES
    nrow8 = lt // SUBLANES
    const = lambda shape: pl.BlockSpec(shape, lambda i, t: (0,) * len(shape))
    head_tile = pl.BlockSpec((1, MLA_HEADS, tm, LANES), lambda i, t: (i, 0, t, 0))
    head_shape = jax.ShapeDtypeStruct((b, MLA_HEADS, lt, LANES), BF16)
    return pl.pallas_call(
        functools.partial(_odd_prep_kernel, nct=nct, ntiles=ntiles, scale=MLA_QK ** -0.5),
        grid=(b, ntiles),
        in_specs=[
            pl.BlockSpec((1, tm, n), lambda i, t: (i, t, 0)),
            pl.BlockSpec((1, SUBLANES, w3), lambda i, t: (i, jnp.maximum(t * per - 1, 0), 0)),
            pl.BlockSpec((1, SUBLANES, w3), lambda i, t: (i, jnp.minimum((t + 1) * per, nrow8 - 1), 0)),
            const((8, w3)), const((8, LANES)), const((1, MLA_Q_RANK)),
            const((MLA_Q_RANK, 2 * MLA_HEADS * LANES)), const((8, LANES)), const((1, MLA_KV_RANK)),
            const((MLA_KV_RANK, 2 * MLA_HEADS * LANES)), const((8, LANES)),
            pl.BlockSpec((tm, LANES), lambda i, t: (t, 0)),
            pl.BlockSpec((tm, LANES), lambda i, t: (t, 0)),
        ],
        out_specs=[
            pl.BlockSpec((1, tm, w3), lambda i, t: (i, t, 0)),
            pl.BlockSpec((1, tm, LANES), lambda i, t: (i, t, 0)),
            head_tile, head_tile, head_tile,
        ],
        out_shape=[
            jax.ShapeDtypeStruct((b, lt, w3), F32),
            jax.ShapeDtypeStruct((b, lt, LANES), F32),
            head_shape, head_shape, head_shape,
        ],
        compiler_params=_cparams(("parallel", "parallel")),
        name="odd_prep",
    )(px, px, px, cw, gvec, qnw, wq, qkw, kvnw, wkv, kkw, ctab, stab)


def _gdn_kernel(qf_ref, kf_ref, vf_ref, gf_ref, qb_ref, kb_ref, vb_ref, gb_ref, of_ref, ob_ref, s_ref):
    @pl.when(pl.program_id(1) == 0)
    def _():
        s_ref[...] = jnp.zeros_like(s_ref)

    c = CHUNK
    nb = 2 * N_HEADS
    heads = lambda ref: [ref[0, :, hd * HEAD_DIM:(hd + 1) * HEAD_DIM] for hd in range(N_HEADS)]
    q = jnp.stack(heads(qf_ref) + heads(qb_ref))
    k = jnp.stack(heads(kf_ref) + heads(kb_ref))
    v = jnp.stack(heads(vf_ref) + heads(vb_ref))
    gf = gf_ref[0]
    gb = gb_ref[0]
    lane_col = lambda gt, j: jnp.broadcast_to(gt[:, j:j + 1], (c, LANES))
    beta = jnp.stack([lane_col(gf, hd) for hd in range(N_HEADS)]
                     + [lane_col(gb, N_HEADS + hd) for hd in range(N_HEADS)])
    la = jnp.stack([lane_col(gf, 2 * N_HEADS + hd) for hd in range(N_HEADS)]
                   + [lane_col(gb, 3 * N_HEADS + hd) for hd in range(N_HEADS)])

    bi = lax.broadcasted_iota(jnp.int32, (nb, c, c), 0)
    ri = lax.broadcasted_iota(jnp.int32, (nb, c, c), 1)
    ci = lax.broadcasted_iota(jnp.int32, (nb, c, c), 2)
    rev = bi >= N_HEADS
    fwd = jnp.logical_not(rev)
    incl = jnp.logical_or(jnp.logical_and(rev, ci >= ri), jnp.logical_and(fwd, ci <= ri))
    strict = jnp.logical_and(incl, ci != ri)
    tri = jnp.where(incl, 1.0, 0.0).astype(BF16)

    g = None
    for part in _split_bf16(la, 3):
        t = lax.dot_general(tri, part, BMM_DIMS, preferred_element_type=F32)
        g = t if g is None else g + t
    gamma = jnp.exp(jnp.where(incl, g - jnp.swapaxes(g, 1, 2), -jnp.inf))
    kq = _bmm(jnp.concatenate([k, q], axis=1), k, BMM_NT_DIMS)
    nm = jnp.where(strict, beta * kq[:, :c] * gamma, 0.0)

    x = None
    for lv in range(int(math.log2(c))):
        same = (ri >> (lv + 1)) == (ci >> (lv + 1))
        r_up = ((ri >> lv) & 1) == 1
        c_up = ((ci >> lv) & 1) == 1
        pick = jnp.logical_and(jnp.logical_xor(r_up, c_up), jnp.logical_xor(r_up, rev))
        blk = jnp.where(jnp.logical_and(same, pick), nm, 0.0)
        if x is None:
            x = jnp.where(ri == ci, 1.0, 0.0) - blk
        else:
            x = x - _bmm(x, _bmm(blk, x))

    eg = jnp.exp(g)
    uw = _bmm(x, jnp.concatenate([beta * v, beta * k * eg], axis=2))
    qk = jnp.where(incl, kq[:, c:] * gamma, 0.0)
    g_tot = jnp.concatenate([g[:N_HEADS, c - 1:c, :], g[N_HEADS:, 0:1, :]], axis=0)
    s = s_ref[...]
    v_new = uw[:, :, :HEAD_DIM] - _bmm(uw[:, :, HEAD_DIM:], s)
    o = _bmm(q * eg, s) + _bmm(qk, v_new)
    s_ref[...] = jnp.exp(g_tot) * s + _bmm(k * jnp.exp(g_tot - g), v_new, BMM_TN_DIMS)
    for hd in range(N_HEADS):
        of_ref[0, :, hd * HEAD_DIM:(hd + 1) * HEAD_DIM] = o[hd]
        ob_ref[0, :, hd * HEAD_DIM:(hd + 1) * HEAD_DIM] = o[N_HEADS + hd]


def _gdn_scan(qkv, gates, nc_ctx):
    b, lt, _ = qkv.shape
    nc = lt // CHUNK
    w = MIX_HALF
    specs = []
    for rev in (False, True):
        order = _chunk_order(nc_ctx, nc, rev)
        specs += [pl.BlockSpec((1, CHUNK, w), lambda i, s, o=order, blk=blk: (i, o(s), blk)) for blk in range(3)]
        specs += [pl.BlockSpec((1, CHUNK, LANES), lambda i, s, o=order: (i, o(s), 0))]
    order_b = _chunk_order(nc_ctx, nc, True)
    return pl.pallas_call(
        _gdn_kernel,
        grid=(b, nc),
        in_specs=specs,
        out_specs=[pl.BlockSpec((1, CHUNK, w), lambda i, s: (i, s, 0)),
                   pl.BlockSpec((1, CHUNK, w), lambda i, s: (i, order_b(s), 0))],
        out_shape=[jax.ShapeDtypeStruct((b, lt, w), F32), jax.ShapeDtypeStruct((b, lt, w), F32)],
        scratch_shapes=[pltpu.VMEM((2 * N_HEADS, HEAD_DIM, HEAD_DIM), F32)],
        compiler_params=_cparams(("parallel", "arbitrary")),
        name="gdn_scan",
    )(qkv, qkv, qkv, gates, qkv, qkv, qkv, gates)


def _attn_kernel(q_ref, k_ref, v_ref, o_ref, m_ref, acc_ref, *, tk, nk):
    m_ref[...] = jnp.full_like(m_ref, -jnp.inf)
    acc_ref[...] = jnp.zeros_like(acc_ref)

    def body(j, carry):
        ks = pl.ds(pl.multiple_of(j * tk, tk), tk)
        for a in range(2):
            s = lax.dot_general(q_ref[0, a], k_ref[0, a, ks, :], NT_DIMS, preferred_element_type=F32)
            m_prev = m_ref[a]
            m_new = jnp.maximum(m_prev, jnp.max(s, axis=-1, keepdims=True))
            p = jnp.exp(s - m_new[:, :1]).astype(BF16)
            acc_ref[a] = jnp.exp(m_prev - m_new) * acc_ref[a] + jnp.dot(
                p, v_ref[0, a, ks, :], preferred_element_type=F32)
            m_ref[a] = m_new
        return carry

    lax.fori_loop(0, nk, body, 0)
    lane = lax.broadcasted_iota(jnp.int32, acc_ref.shape[1:], 1)
    a0 = acc_ref[0]
    a1 = acc_ref[1]
    o_ref[0] = jnp.where(lane < MLA_V, a0 / pltpu.roll(a0, MLA_V, 1), pltpu.roll(a1, MLA_V, 1) / a1)


def _attention(q, k, v, n_ctx):
    b, nh, lt, _ = q.shape
    lq = lt - n_ctx
    tq = next(c for c in ATTN_TQ_CANDIDATES if lq % c == 0 and n_ctx % c == 0)
    tk = next(c for c in ATTN_TK_CANDIDATES if lt % c == 0)
    q_off = n_ctx // tq
    return pl.pallas_call(
        functools.partial(_attn_kernel, tk=tk, nk=lt // tk),
        grid=(b, nh // 2, lq // tq),
        in_specs=[
            pl.BlockSpec((1, 2, tq, LANES), lambda i, h, qi: (i, h, qi + q_off, 0)),
            pl.BlockSpec((1, 2, lt, LANES), lambda i, h, qi: (i, h, 0, 0)),
            pl.BlockSpec((1, 2, lt, LANES), lambda i, h, qi: (i, h, 0, 0)),
        ],
        out_specs=pl.BlockSpec((1, tq, 2 * MLA_V), lambda i, h, qi: (i, qi, h)),
        out_shape=jax.ShapeDtypeStruct((b, lq, nh * MLA_V), F32),
        scratch_shapes=[pltpu.VMEM((2, tq, LANES), F32), pltpu.VMEM((2, tq, LANES), F32)],
        compiler_params=_cparams(("parallel", "parallel", "arbitrary")),
        name="mla_attention",
    )(q, k, v)


def _odd_merge_kernel(z_ref, of_ref, ob_ref, att_ref, x_ref, mod_ref, gnw_ref, wo_ref, o_ref, cat_ref):
    o = of_ref[0] + ob_ref[0]
    gate = _silu(z_ref[0])
    for hd in range(N_HEADS):
        cs = slice(hd * HEAD_DIM, (hd + 1) * HEAD_DIM)
        cat_ref[:, cs] = (_rms(o[:, cs], gnw_ref[...]) * gate[:, cs]).astype(BF16)
    cat_ref[:, MIX_HALF:] = att_ref[0].astype(BF16)
    y = jnp.dot(cat_ref[...], wo_ref[...], preferred_element_type=F32)
    o_ref[0] = x_ref[0] + mod_ref[0, 0][2:3] * y


def _odd_merge(px, o_f, o_b, att, xa, modtab, gnw, wo_bf16, nct):
    b, lt, d = xa.shape
    tm = TOKEN_TILE
    lq = att.shape[1]
    w = MIX_HALF
    lat = lambda width, blk: pl.BlockSpec((1, tm, width), lambda i, t: (i, t + nct, blk))
    const = lambda shape: pl.BlockSpec(shape, lambda i, t: (0,) * len(shape))
    return pl.pallas_call(
        _odd_merge_kernel,
        grid=(b, lq // tm),
        in_specs=[
            lat(w, 3), lat(w, 0), lat(w, 0),
            pl.BlockSpec((1, tm, w), lambda i, t: (i, t, 0)),
            lat(d, 0),
            pl.BlockSpec((1, 1, 8, d), lambda i, t: (i, 1, 0, 0)),
            const((1, HEAD_DIM)), const((2 * w, d)),
        ],
        out_specs=lat(d, 0),
        out_shape=jax.ShapeDtypeStruct((b, lt, d), F32),
        scratch_shapes=[pltpu.VMEM((tm, 2 * w), BF16)],
        input_output_aliases={4: 0},
        compiler_params=_cparams(("parallel", "parallel")),
        name="odd_merge",
    )(px, o_f, o_b, att, xa, modtab, gnw, wo_bf16)


def _router_kernel(x_ref, mod_ref, nw_ref, wr_ref, br_ref, h_ref, info_ref, cnt_ref, run_ref):
    @pl.when(jnp.logical_and(pl.program_id(0) == 0, pl.program_id(1) == 0))
    def _():
        run_ref[...] = jnp.zeros_like(run_ref)

    tm = x_ref.shape[1]
    mod = mod_ref[0, 0]
    h = _rms(x_ref[0], nw_ref[...]) * (1.0 + mod[4:5]) + mod[3:4]
    h_ref[0] = h
    logits = _dot3(h, wr_ref[...]) + br_ref[...]
    lane = lax.broadcasted_iota(jnp.int32, (tm, LANES), 1)
    lane_f = lane.astype(F32)
    big = float(LANES)
    neg = -jnp.inf
    is_g = jnp.logical_and(lane >= N_EXPERTS, lane < N_EXPERTS + N_GROUPS)
    lg = jnp.where(is_g, logits, neg)
    mg = jnp.max(lg, axis=-1, keepdims=True)
    pg_top = 1.0 / jnp.sum(jnp.exp(lg - mg), axis=-1, keepdims=True)
    g_idx = jnp.min(jnp.where(lg == mg, lane_f - N_EXPERTS, big), axis=-1, keepdims=True)
    in_g = jnp.logical_and(lane < N_EXPERTS, (lane >> 3).astype(F32) == g_idx)
    le = jnp.where(in_g, logits, neg)
    m1 = jnp.max(le, axis=-1, keepdims=True)
    e1 = jnp.min(jnp.where(le == m1, lane_f, big), axis=-1, keepdims=True)
    le2 = jnp.where(lane_f == e1, neg, le)
    m2 = jnp.max(le2, axis=-1, keepdims=True)
    e2 = jnp.min(jnp.where(le2 == m2, lane_f, big), axis=-1, keepdims=True)
    se = jnp.sum(jnp.exp(le - m1), axis=-1, keepdims=True)
    p1 = 1.0 / se
    p2 = jnp.exp(m2 - m1) / se
    w1 = pg_top * p1 / (p1 + p2)
    w2 = pg_top * p2 / (p1 + p2)
    hit1 = lane_f == e1
    hit2 = lane_f == e2
    oh = jnp.where(jnp.logical_or(hit1, hit2), 1.0, 0.0)
    r = lax.broadcasted_iota(jnp.int32, (tm, tm), 0)
    c = lax.broadcasted_iota(jnp.int32, (tm, tm), 1)
    before = jnp.dot(jnp.where(c < r, 1.0, 0.0).astype(BF16), oh.astype(BF16), preferred_element_type=F32)
    base = before + run_ref[0:1, :]
    pos1 = jnp.sum(jnp.where(hit1, base, 0.0), axis=-1, keepdims=True)
    pos2 = jnp.sum(jnp.where(hit2, base, 0.0), axis=-1, keepdims=True)
    run = run_ref[0:1, :] + jnp.sum(oh, axis=0, keepdims=True)
    run_ref[...] = jnp.broadcast_to(run, run_ref.shape)
    cnt_ref[...] = jnp.broadcast_to(run, cnt_ref.shape)
    info = jnp.zeros((tm, LANES), F32)
    for j, val in enumerate((e1, e2, pos1, pos2, w1, w2)):
        info = jnp.where(lane == j, val, info)
    info_ref[0] = info


def _router(xa, modtab, norm_w, wr, br, nct):
    b, lt, d = xa.shape
    tm = TOKEN_TILE
    const = lambda shape: pl.BlockSpec(shape, lambda i, t: (0,) * len(shape))
    return pl.pallas_call(
        _router_kernel,
        grid=(b, lt // tm),
        in_specs=[
            pl.BlockSpec((1, tm, d), lambda i, t: (i, t, 0)),
            pl.BlockSpec((1, 1, 8, d), _mod_index(nct)),
            const((1, d)), const((d, LANES)), const((1, LANES)),
        ],
        out_specs=[
            pl.BlockSpec((1, tm, d), lambda i, t: (i, t, 0)),
            pl.BlockSpec((1, tm, LANES), lambda i, t: (i, t, 0)),
            const((SUBLANES, LANES)),
        ],
        out_shape=[
            jax.ShapeDtypeStruct((b, lt, d), F32),
            jax.ShapeDtypeStruct((b, lt, LANES), F32),
            jax.ShapeDtypeStruct((SUBLANES, LANES), F32),
        ],
        scratch_shapes=[pltpu.VMEM((SUBLANES, LANES), F32)],
        compiler_params=_cparams(("arbitrary", "arbitrary")),
        name="moe_router",
    )(xa, modtab, norm_w.reshape(1, d), wr, br)


def _dispatch_kernel(dest_ref, h_ref, xs_in_ref, xs_ref, sem):
    del xs_in_ref
    tm = h_ref.shape[0]

    def row_copy(r, k):
        return pltpu.make_async_copy(h_ref.at[pl.ds(r, 1)], xs_ref.at[pl.ds(dest_ref[0, 0, 2 * r + k], 1)], sem)

    def issue(r, carry):
        row_copy(r, 0).start(priority=0)
        row_copy(r, 1).start(priority=1)
        return carry

    def drain(r, carry):
        row_copy(r, 0).wait()
        row_copy(r, 1).wait()
        return carry

    lax.fori_loop(0, tm, issue, 0, unroll=ROW_DMA_UNROLL)
    lax.fori_loop(0, tm, drain, 0, unroll=ROW_DMA_UNROLL)


def _dispatch(dest3, h2d, n_slots):
    t, d = h2d.shape
    tm = TOKEN_TILE
    xs0 = jnp.zeros((n_slots, d), F32)
    return pl.pallas_call(
        _dispatch_kernel,
        grid=(t // tm,),
        in_specs=[
            pl.BlockSpec((1, 1, 2 * tm), lambda i: (i, 0, 0), memory_space=pltpu.SMEM),
            pl.BlockSpec((tm, d), lambda i: (i, 0)),
            pl.BlockSpec(memory_space=pl.ANY),
        ],
        out_specs=pl.BlockSpec(memory_space=pl.ANY),
        out_shape=jax.ShapeDtypeStruct((n_slots, d), F32),
        scratch_shapes=[pltpu.SemaphoreType.DMA],
        input_output_aliases={2: 0},
        compiler_params=_cparams(("arbitrary",)),
        name="moe_dispatch",
    )(dest3, h2d, xs0)


def _expert_kernel(be_ref, nu_ref, xs_ref, wg_ref, wu_ref, wd_ref, ys_ref):
    i = pl.program_id(0)

    @pl.when(i < nu_ref[0])
    def _():
        x = xs_ref[...].astype(BF16)
        a = jnp.dot(x, wg_ref[0], preferred_element_type=F32)
        u = jnp.dot(x, wu_ref[0], preferred_element_type=F32)
        ys_ref[...] = jnp.dot((_silu(a) * u).astype(BF16), wd_ref[0], preferred_element_type=F32)

    @pl.when(i >= nu_ref[0])
    def _():
        ys_ref[...] = jnp.zeros_like(ys_ref)


def _experts(blk_e, n_used, xs, wg, wu, wd):
    n_slots, d = xs.shape
    ff = wg.shape[2]
    nb = n_slots // MOE_BLOCK
    grid_spec = pltpu.PrefetchScalarGridSpec(
        num_scalar_prefetch=2,
        grid=(nb,),
        in_specs=[
            pl.BlockSpec((MOE_BLOCK, d), lambda i, be, nu: (jnp.minimum(i, jnp.maximum(nu[0] - 1, 0)), 0)),
            pl.BlockSpec((1, d, ff), lambda i, be, nu: (be[i], 0, 0)),
            pl.BlockSpec((1, d, ff), lambda i, be, nu: (be[i], 0, 0)),
            pl.BlockSpec((1, ff, d), lambda i, be, nu: (be[i], 0, 0)),
        ],
        out_specs=pl.BlockSpec((MOE_BLOCK, d), lambda i, be, nu: (i, 0)),
    )
    return pl.pallas_call(
        _expert_kernel,
        grid_spec=grid_spec,
        out_shape=jax.ShapeDtypeStruct((n_slots, d), F32),
        compiler_params=_cparams(("arbitrary",)),
        name="moe_experts",
    )(blk_e, n_used, xs, wg, wu, wd)


def _combine_kernel(dest_ref, info_ref, x_ref, mod_ref, ys_ref, o_ref, buf_ref, sem):
    tm = x_ref.shape[1]

    def row_copy(r, k):
        return pltpu.make_async_copy(ys_ref.at[pl.ds(dest_ref[0, 0, 2 * r + k], 1)],
                                     buf_ref.at[k, pl.ds(r, 1)], sem)

    def issue(r, carry):
        row_copy(r, 0).start(priority=0)
        row_copy(r, 1).start(priority=1)
        return carry

    def drain(r, carry):
        row_copy(r, 0).wait()
        row_copy(r, 1).wait()
        return carry

    lax.fori_loop(0, tm, issue, 0, unroll=ROW_DMA_UNROLL)
    lax.fori_loop(0, tm, drain, 0, unroll=ROW_DMA_UNROLL)
    info = info_ref[0]
    y = info[:, 4:5] * buf_ref[0] + info[:, 5:6] * buf_ref[1]
    o_ref[0] = x_ref[0] + mod_ref[0, 0][5:6] * y


def _combine(dest3, info, xa, modtab, ys, nct, latent_only):
    b, lt, d = xa.shape
    tm = TOKEN_TILE
    nt = lt // tm
    skip = nct if latent_only else 0
    mod_map = _mod_index(nct)
    return pl.pallas_call(
        _combine_kernel,
        grid=(b, nt - skip),
        in_specs=[
            pl.BlockSpec((1, 1, 2 * tm), lambda i, t: (i * nt + t + skip, 0, 0), memory_space=pltpu.SMEM),
            pl.BlockSpec((1, tm, LANES), lambda i, t: (i, t + skip, 0)),
            pl.BlockSpec((1, tm, d), lambda i, t: (i, t + skip, 0)),
            pl.BlockSpec((1, 1, 8, d), lambda i, t: mod_map(i, t + skip)),
            pl.BlockSpec(memory_space=pl.ANY),
        ],
        out_specs=pl.BlockSpec((1, tm, d), lambda i, t: (i, t, 0)),
        out_shape=jax.ShapeDtypeStruct((b, lt - skip * tm, d), F32),
        scratch_shapes=[pltpu.VMEM((2, tm, d), F32), pltpu.SemaphoreType.DMA],
        input_output_aliases={} if latent_only else {2: 0},
        compiler_params=_cparams(("arbitrary", "arbitrary")),
        name="moe_combine",
    )(dest3, info, xa, modtab, ys)


def _moe(xa, modtab, norm_w, wg_r, bg_r, we_r, be_r, w_gate, w_up, w_down, nct, latent_only=False):
    b, lt, d = xa.shape
    t = b * lt
    tm = TOKEN_TILE
    pad = LANES - N_EXPERTS - N_GROUPS
    wr = jnp.concatenate([we_r, wg_r, jnp.zeros((d, pad), F32)], axis=1)
    br = jnp.concatenate([be_r, bg_r, jnp.zeros((pad,), F32)]).reshape(1, LANES)
    h, info, cnt = _router(xa, modtab, norm_w, wr, br, nct)
    counts = cnt[0, :N_EXPERTS].astype(jnp.int32)
    padded = (counts + MOE_BLOCK - 1) // MOE_BLOCK * MOE_BLOCK
    pend = jnp.cumsum(padded)
    pstart = pend - padded
    info2 = info.reshape(t, LANES)
    e_idx = info2[:, 0:2].astype(jnp.int32)
    dest = pstart[e_idx] + info2[:, 2:4].astype(jnp.int32)
    dest3 = dest.reshape(t // tm, 1, 2 * tm)
    n_blk = -(-2 * t // MOE_BLOCK) + N_EXPERTS
    blk_start = jnp.arange(n_blk, dtype=jnp.int32) * MOE_BLOCK
    blk_e = jnp.minimum(jnp.sum((pend[None, :] <= blk_start[:, None]).astype(jnp.int32), axis=1), N_EXPERTS - 1)
    n_used = (pend[-1:] // MOE_BLOCK).astype(jnp.int32)
    xs = _dispatch(dest3, h.reshape(t, d), n_blk * MOE_BLOCK)
    ys = _experts(blk_e, n_used, xs, w_gate.astype(BF16), w_up.astype(BF16), w_down.astype(BF16))
    return _combine(dest3, info, xa, modtab, ys, nct, latent_only)


_ROPE_SWAP = np.concatenate([np.arange(8, 16), np.arange(0, 8), np.arange(24, 32), np.arange(16, 24)])


def _odd_in_weights(w_in):
    d = w_in.shape[0]
    w4 = 4 * MIX_HALF
    o_beta = w4
    o_cq = o_beta + 4 * N_HEADS
    o_ckv = o_cq + MLA_Q_RANK
    o_kr = o_ckv + MLA_KV_RANK
    z = lambda n: jnp.zeros((d, n), F32)
    kr = w_in[:, o_kr:o_kr + MLA_ROPE]
    blocks = [
        w_in[:, :w4], w_in[:, o_cq:o_ckv], w_in[:, o_ckv:o_kr],
        z(MLA_NOPE), kr, z(LANES - MLA_QK),
        z(MLA_NOPE), kr[:, _ROPE_SWAP], z(LANES - MLA_QK),
        w_in[:, o_beta:o_cq], z(LANES - 4 * N_HEADS),
    ]
    return jnp.concatenate(blocks, axis=1)


def _per_head_lanes(w, per, take, width):
    r = w.shape[0]
    wh = w.reshape(r, -1, per)[:, :, take]
    return jnp.pad(wh, ((0, 0), (0, 0), (0, width - wh.shape[2]))).reshape(r, -1)


def _rope_lane_weights(w):
    plain = jnp.pad(w, (0, LANES - MLA_QK))
    sw = jnp.pad(jnp.concatenate([jnp.zeros((MLA_NOPE,), F32), w[MLA_NOPE:][_ROPE_SWAP]]), (0, LANES - MLA_QK))
    return jnp.pad(jnp.stack([plain, sw]), ((0, 6), (0, 0)))


def _rope_tables(n_ctx, n_lat):
    quarter = MLA_ROPE // 4
    freqs = ROPE_THETA ** (-jnp.arange(quarter, dtype=F32) / quarter)
    pos = jnp.arange(n_lat)
    row = (pos // GRID_W).astype(F32)[:, None] * freqs
    col = (pos % GRID_W).astype(F32)[:, None] * freqs
    ones = jnp.ones((n_lat, MLA_NOPE), F32)
    zpad = jnp.zeros((n_lat, LANES - MLA_QK), F32)
    ctab = jnp.concatenate([ones, jnp.cos(row), jnp.cos(row), jnp.cos(col), jnp.cos(col), zpad], axis=1)
    stab = jnp.concatenate([0 * ones, -jnp.sin(row), jnp.sin(row), -jnp.sin(col), jnp.sin(col), zpad], axis=1)
    c_ctx = jnp.concatenate([jnp.ones((n_ctx, MLA_QK), F32), jnp.zeros((n_ctx, LANES - MLA_QK), F32)], axis=1)
    return jnp.concatenate([c_ctx, ctab], axis=0), jnp.concatenate([jnp.zeros((n_ctx, LANES), F32), stab], axis=0)


def _lower_bound(logits, layer):
    return jnp.cumsum(jax.nn.softmax(logits.astype(F32), axis=0), axis=0)[layer]


def kernel(x, c, ctx, c_ctx, ada_w, ada_b, norm_mix_w, norm_ffn_w, even_w_in, even_w_out, gmlp_norm_w, gmlp_ws, gmlp_bs, hgrn_lb_logits, hgrn_norm_w, odd_w_in, odd_w_out, gdn_conv_w, gdn_a_log, gdn_dt_bias, gdn_norm_w, mla_q_norm_w, mla_wq_up, mla_kv_norm_w, mla_wkv_up, mla_qk_norm_q, mla_qk_norm_k, router_group_w, router_group_b, router_expert_w, router_expert_b, moe_w_gate, moe_w_up, moe_w_down):
    b, n_lat, d = x.shape
    n_ctx = ctx.shape[1]
    depth = ada_w.shape[0]
    tm = TOKEN_TILE
    assert depth == 2 and n_ctx % tm == 0 and n_lat % tm == 0 and n_lat % GRID_W == 0
    nct = n_ctx // tm
    ncc = n_ctx // CHUNK

    xa = jnp.concatenate([ctx, x], axis=1)
    rows = -(-(b + 1) // SUBLANES) * SUBLANES
    cvec = jnp.zeros((rows, d), F32).at[:b].set(c).at[b].set(c_ctx)
    mods = _mod_vectors(cvec, ada_w, ada_b).reshape(depth, rows, 6, d)

    def mod_table(layer):
        mx = mods[layer, :b]
        mc = jnp.broadcast_to(mods[layer, b], (b, 6, d))
        return jnp.pad(jnp.stack([mc, mx], axis=1), ((0, 0), (0, 0), (0, 2), (0, 0)))

    modtab = mod_table(0)
    px = _in_projection(xa, modtab, norm_mix_w[0], even_w_in[0].astype(BF16), nct)
    lb_f = _lower_bound(hgrn_lb_logits[0], 0).reshape(1, MIX_HALF)
    lb_b = _lower_bound(hgrn_lb_logits[1], 0).reshape(1, MIX_HALF)
    o_f, o_b = _hgrn_scan(px, lb_f, lb_b, ncc)
    bsb = jnp.broadcast_to(gmlp_bs[0][:, :, None], (N_HEADS, CHUNK, HEAD_DIM))
    xa = _even_merge(px, o_f, o_b, xa, modtab, gmlp_norm_w[0].reshape(1, MIX_HALF), gmlp_ws[0].astype(BF16), bsb,
                     hgrn_norm_w[0].reshape(1, MIX_HALF), even_w_out[0].astype(BF16), nct)
    xa = _moe(xa, modtab, norm_ffn_w[0], router_group_w[0], router_group_b[0], router_expert_w[0],
              router_expert_b[0], moe_w_gate[0], moe_w_up[0], moe_w_down[0], nct)

    modtab = mod_table(1)
    px = _in_projection(xa, modtab, norm_mix_w[1], _odd_in_weights(odd_w_in[0]).astype(BF16), nct)
    cw = jnp.pad(gdn_conv_w[0], ((0, 8 - GDN_CONV), (0, 0)))
    gvec = jnp.zeros((8, LANES), F32)
    gvec = gvec.at[0, 2 * N_HEADS:4 * N_HEADS].set(gdn_a_log[0].reshape(-1))
    gvec = gvec.at[1, 2 * N_HEADS:4 * N_HEADS].set(gdn_dt_bias[0].reshape(-1))
    wq = mla_wq_up[0]
    rope_cols = MLA_NOPE + _ROPE_SWAP
    wq_all = jnp.concatenate([
        _per_head_lanes(wq, MLA_QK, np.arange(MLA_QK), LANES),
        jnp.pad(_per_head_lanes(wq, MLA_QK, rope_cols, LANES - MLA_NOPE).reshape(MLA_Q_RANK, MLA_HEADS, -1),
                ((0, 0), (0, 0), (MLA_NOPE, 0)))[:, :, :LANES].reshape(MLA_Q_RANK, -1),
    ], axis=1).astype(BF16)
    wkv = mla_wkv_up[0]
    wkv_all = jnp.concatenate([
        _per_head_lanes(wkv, MLA_NOPE + MLA_V, np.arange(MLA_NOPE), LANES),
        _per_head_lanes(wkv, MLA_NOPE + MLA_V, MLA_NOPE + np.arange(MLA_V), LANES),
    ], axis=1).astype(BF16)
    ctab, stab = _rope_tables(n_ctx, n_lat)
    qkv, gates, q, k, v = _odd_prep(px, cw, gvec, mla_q_norm_w[0].reshape(1, -1), wq_all,
                                    _rope_lane_weights(mla_qk_norm_q[0]), mla_kv_norm_w[0].reshape(1, -1), wkv_all,
                                    _rope_lane_weights(mla_qk_norm_k[0]), ctab, stab, nct)
    o_f, o_b = _gdn_scan(qkv, gates, ncc)
    att = _attention(q, k, v, n_ctx)
    xa = _odd_merge(px, o_f, o_b, att, xa, modtab, gdn_norm_w[0].reshape(1, HEAD_DIM), odd_w_out[0].astype(BF16), nct)
    return _moe(xa, modtab, norm_ffn_w[1], router_group_w[1], router_group_b[1], router_expert_w[1],
                router_expert_b[1], moe_w_gate[1], moe_w_up[1], moe_w_down[1], nct, latent_only=True)
```

```python
import functools
import math

import jax
import jax.numpy as jnp
import numpy as np
from jax import lax
from jax.experimental import pallas as pl
from jax.experimental.pallas import tpu as pltpu

F32 = jnp.float32
BF16 = jnp.bfloat16
EPS = 1e-6

LANES = 128
SUBLANES = 8
TOKEN_TILE = 256
CHUNK = 128
VMEM_LIMIT = 56 * 1024 * 1024

GRID_W = 64
ROPE_THETA = 10000.0
HEAD_DIM = 128
N_HEADS = 4
MIX_HALF = N_HEADS * HEAD_DIM
GDN_CONV = 5
MLA_HEADS = 8
MLA_NOPE = 64
MLA_ROPE = 32
MLA_V = 64
MLA_QK = MLA_NOPE + MLA_ROPE
MLA_Q_RANK = 256
MLA_KV_RANK = 128
N_GROUPS = 4
EXPERTS_PER_GROUP = 8
N_EXPERTS = N_GROUPS * EXPERTS_PER_GROUP
MOE_BLOCK = 256
ROW_DMA_UNROLL = 16
ATTN_TQ_CANDIDATES = (512, 256)
ATTN_TK_CANDIDATES = (768, 512, 384, 256, 128)

NT_DIMS = (((1,), (1,)), ((), ()))
TN_DIMS = (((0,), (0,)), ((), ()))


def _cparams(sem):
    return pltpu.CompilerParams(dimension_semantics=sem, vmem_limit_bytes=VMEM_LIMIT)


def _sigmoid(x):
    return 1.0 / (1.0 + jnp.exp(-x))


def _silu(x):
    return x * _sigmoid(x)


def _gelu_tanh(x):
    c = math.sqrt(2.0 / math.pi)
    return 0.5 * x * (1.0 + jnp.tanh(c * (x + 0.044715 * (x * x * x))))


def _softplus(x):
    return jnp.maximum(x, 0.0) + jnp.log(1.0 + jnp.exp(-jnp.abs(x)))


def _bdot(a, b):
    return jnp.dot(a.astype(BF16), b.astype(BF16), preferred_element_type=F32)


def _bdot_nt(a, b):
    return lax.dot_general(a.astype(BF16), b.astype(BF16), NT_DIMS, preferred_element_type=F32)


def _bdot_tn(a, b):
    return lax.dot_general(a.astype(BF16), b.astype(BF16), TN_DIMS, preferred_element_type=F32)


BMM_DIMS = (((2,), (1,)), ((0,), (0,)))
BMM_NT_DIMS = (((2,), (2,)), ((0,), (0,)))
BMM_TN_DIMS = (((1,), (1,)), ((0,), (0,)))


def _bmm(a, b, dims=BMM_DIMS):
    return lax.dot_general(a.astype(BF16), b.astype(BF16), dims, preferred_element_type=F32)


def _split_bf16(x, n):
    parts = []
    r = x
    for _ in range(n):
        p = r.astype(BF16)
        parts.append(p)
        r = r - p.astype(F32)
    return parts


def _dot_exact_lhs(m01, x):
    acc = None
    for p in _split_bf16(x, 3):
        t = jnp.dot(m01, p, preferred_element_type=F32)
        acc = t if acc is None else acc + t
    return acc


def _dot3(a, b):
    a_hi, a_lo = _split_bf16(a, 2)
    b_hi, b_lo = _split_bf16(b, 2)
    d = lambda u, v: jnp.dot(u, v, preferred_element_type=F32)
    return d(a_hi, b_hi) + (d(a_hi, b_lo) + d(a_lo, b_hi))


def _rms(x, w, n=None):
    n = x.shape[-1] if n is None else n
    ms = jnp.sum(x * x, axis=-1, keepdims=True) * (1.0 / n)
    return x * lax.rsqrt(ms + EPS) * w


def _mod_kernel(c_ref, w_ref, b_ref, o_ref):
    s = _silu(c_ref[...])
    o_ref[0] = _dot3(s, w_ref[0]) + b_ref[0]


def _mod_vectors(cvec, ada_w, ada_b):
    depth, d, n = ada_w.shape
    rows = cvec.shape[0]
    tn = 512
    return pl.pallas_call(
        _mod_kernel,
        grid=(depth, n // tn),
        in_specs=[
            pl.BlockSpec((rows, d), lambda l, j: (0, 0)),
            pl.BlockSpec((1, d, tn), lambda l, j: (l, 0, j)),
            pl.BlockSpec((1, 1, tn), lambda l, j: (l, 0, j)),
        ],
        out_specs=pl.BlockSpec((1, rows, tn), lambda l, j: (l, 0, j)),
        out_shape=jax.ShapeDtypeStruct((depth, rows, n), F32),
        compiler_params=_cparams(("parallel", "parallel")),
        name="adaln_mod",
    )(cvec, ada_w, ada_b.reshape(depth, 1, n))


def _mod_index(nct):
    return lambda b, t: (b, jnp.where(t < nct, 0, 1), 0, 0)


def _inproj_kernel(x_ref, mod_ref, nw_ref, w_ref, o_ref, *, shift_row, tn):
    mod = mod_ref[0, 0]
    h = _rms(x_ref[0], nw_ref[...]) * (1.0 + mod[shift_row + 1:shift_row + 2]) + mod[shift_row:shift_row + 1]
    hb = h.astype(BF16)
    for j in range(w_ref.shape[1] // tn):
        o_ref[0, :, j * tn:(j + 1) * tn] = jnp.dot(hb, w_ref[:, j * tn:(j + 1) * tn], preferred_element_type=F32)


def _in_projection(xa, modtab, norm_w, w_bf16, nct):
    b, lt, d = xa.shape
    n = w_bf16.shape[1]
    tm = TOKEN_TILE
    tn = 512 if n % 512 == 0 else 256
    return pl.pallas_call(
        functools.partial(_inproj_kernel, shift_row=0, tn=tn),
        grid=(b, lt // tm),
        in_specs=[
            pl.BlockSpec((1, tm, d), lambda i, t: (i, t, 0)),
            pl.BlockSpec((1, 1, 8, d), _mod_index(nct)),
            pl.BlockSpec((1, d), lambda i, t: (0, 0)),
            pl.BlockSpec((d, n), lambda i, t: (0, 0)),
        ],
        out_specs=pl.BlockSpec((1, tm, n), lambda i, t: (i, t, 0)),
        out_shape=jax.ShapeDtypeStruct((b, lt, n), F32),
        compiler_params=_cparams(("parallel", "parallel")),
        name="in_projection",
    )(xa, modtab, norm_w.reshape(1, d), w_bf16)


def _chunk_order(nc_ctx, nc_all, rev):
    if not rev:
        return lambda s: s
    return lambda s: jnp.where(s < nc_ctx, nc_ctx - 1 - s, nc_all - 1 - (s - nc_ctx))


def _tri01(n, rev):
    r = lax.broadcasted_iota(jnp.int32, (n, n), 0)
    c = lax.broadcasted_iota(jnp.int32, (n, n), 1)
    keep = (c >= r) if rev else (c <= r)
    return jnp.where(keep, 1.0, 0.0).astype(BF16)


def _level_ref_rows(g_ref, lo, hi, half, rev):
    seg = 2 * half
    nseg = CHUNK // seg
    nb = hi - lo
    row_of = lambda s: s * seg + half - (0 if rev else 1)
    if seg >= SUBLANES:
        pieces = [jnp.broadcast_to(g_ref[lo:hi, pl.ds(row_of(s), 1), :], (nb, seg, LANES)) for s in range(nseg)]
        return jnp.concatenate(pieces, axis=1)
    sub = lax.broadcasted_iota(jnp.int32, (nb, SUBLANES, LANES), 1) >> int(math.log2(seg))
    per = SUBLANES // seg
    pieces = []
    for v in range(CHUNK // SUBLANES):
        acc = jnp.broadcast_to(g_ref[lo:hi, pl.ds(row_of(v * per), 1), :], (nb, SUBLANES, LANES))
        for j in range(1, per):
            cand = jnp.broadcast_to(g_ref[lo:hi, pl.ds(row_of(v * per + j), 1), :], (nb, SUBLANES, LANES))
            acc = jnp.where(sub >= j, cand, acc)
        pieces.append(acc)
    return jnp.concatenate(pieces, axis=1)


def _hgrn_kernel(qf_ref, vf_ref, ff_ref, qb_ref, vb_ref, fb_ref, lbf_ref, lbb_ref, of_ref, ob_ref, st_ref, g_ref):
    @pl.when(pl.program_id(1) == 0)
    def _():
        st_ref[...] = jnp.zeros_like(st_ref)

    c = CHUNK
    nb = 2 * N_HEADS
    heads = lambda ref, lead: [ref[lead, :, hd * HEAD_DIM:(hd + 1) * HEAD_DIM] for hd in range(N_HEADS)]
    q = jnp.stack(heads(qf_ref, 0) + heads(qb_ref, 0))
    v = jnp.stack(heads(vf_ref, 0) + heads(vb_ref, 0))
    fr = jnp.stack(heads(ff_ref, 0) + heads(fb_ref, 0))
    lb = jnp.stack([lbf_ref[:, hd * HEAD_DIM:(hd + 1) * HEAD_DIM] for hd in range(N_HEADS)]
                   + [lbb_ref[:, hd * HEAD_DIM:(hd + 1) * HEAD_DIM] for hd in range(N_HEADS)])
    f = lb + (1.0 - lb) * _sigmoid(fr)
    k = 1.0 - f

    bi = lax.broadcasted_iota(jnp.int32, (nb, c, c), 0)
    ri = lax.broadcasted_iota(jnp.int32, (nb, c, c), 1)
    ci = lax.broadcasted_iota(jnp.int32, (nb, c, c), 2)
    rev = bi >= N_HEADS
    incl = jnp.logical_or(jnp.logical_and(rev, ci >= ri), jnp.logical_and(jnp.logical_not(rev), ci <= ri))
    tri = jnp.where(incl, 1.0, 0.0).astype(BF16)
    g = None
    for part in _split_bf16(jnp.log(f), 3):
        t = lax.dot_general(tri, part, BMM_DIMS, preferred_element_type=F32)
        g = t if g is None else g + t
    g_ref[...] = g

    a = jnp.where(ri == ci, jnp.sum(q * k, axis=-1, keepdims=True), 0.0)
    for lv in range(int(math.log2(c))):
        half = 1 << lv
        r = jnp.concatenate([_level_ref_rows(g_ref, 0, N_HEADS, half, False),
                             _level_ref_rows(g_ref, N_HEADS, nb, half, True)], axis=0)
        e = jnp.exp(-jnp.abs(g - r))
        same = (ri >> (lv + 1)) == (ci >> (lv + 1))
        q_late = jnp.logical_xor((ri & half) != 0, rev)
        k_early = jnp.logical_xor((ci & half) == 0, rev)
        keep = jnp.logical_and(same, jnp.logical_and(q_late, k_early))
        a = a + jnp.where(keep, _bmm(q * e, k * e, BMM_NT_DIMS), 0.0)
    g_tot = jnp.concatenate([g_ref[0:N_HEADS, pl.ds(c - 1, 1), :], g_ref[N_HEADS:nb, pl.ds(0, 1), :]], axis=0)
    st = st_ref[...]
    o = _bmm(a, v) + _bmm(q * jnp.exp(g), st, BMM_NT_DIMS)
    st_ref[...] = st * jnp.exp(g_tot) + _bmm(v, k * jnp.exp(g_tot - g), BMM_TN_DIMS)
    for hd in range(N_HEADS):
        of_ref[0, :, hd * HEAD_DIM:(hd + 1) * HEAD_DIM] = o[hd]
        ob_ref[0, :, hd * HEAD_DIM:(hd + 1) * HEAD_DIM] = o[N_HEADS + hd]


def _hgrn_scan(px, lb_f, lb_b, nc_ctx):
    b, lt, _ = px.shape
    nc = lt // CHUNK
    w = MIX_HALF
    order_b = _chunk_order(nc_ctx, nc, True)
    fwd = lambda blk: pl.BlockSpec((1, CHUNK, w), lambda i, s: (i, s, blk))
    bwd = lambda blk: pl.BlockSpec((1, CHUNK, w), lambda i, s: (i, order_b(s), blk))
    const = pl.BlockSpec((1, w), lambda i, s: (0, 0))
    return pl.pallas_call(
        _hgrn_kernel,
        grid=(b, nc),
        in_specs=[fwd(2), fwd(3), fwd(4), bwd(2), bwd(3), bwd(5), const, const],
        out_specs=[fwd(0), bwd(0)],
        out_shape=[jax.ShapeDtypeStruct((b, lt, w), F32), jax.ShapeDtypeStruct((b, lt, w), F32)],
        scratch_shapes=[pltpu.VMEM((2 * N_HEADS, HEAD_DIM, HEAD_DIM), F32),
                        pltpu.VMEM((2 * N_HEADS, CHUNK, HEAD_DIM), F32)],
        compiler_params=_cparams(("parallel", "arbitrary")),
        name="hgrn_scan",
    )(px, px, px, px, px, px, lb_f, lb_b)


def _even_merge_kernel(pa_ref, pg_ref, of_ref, ob_ref, x_ref, mod_ref, gnw_ref, ws_ref, bsb_ref,
                       hnw_ref, wo_ref, o_ref, cat_ref):
    tm = pa_ref.shape[1]
    z = _gelu_tanh(pa_ref[0])
    u = z[:, :MIX_HALF]
    v = _rms(z[:, MIX_HALF:], gnw_ref[...])
    for cc in range(tm // CHUNK):
        rs = slice(cc * CHUNK, (cc + 1) * CHUNK)
        for g in range(N_HEADS):
            cs = slice(g * HEAD_DIM, (g + 1) * HEAD_DIM)
            s = _bdot(ws_ref[g], v[rs, cs]) + bsb_ref[g]
            cat_ref[rs, cs] = (u[rs, cs] * s).astype(BF16)
    o = of_ref[0] + ob_ref[0]
    gate = _silu(pg_ref[0])
    for hd in range(N_HEADS):
        cs = slice(hd * HEAD_DIM, (hd + 1) * HEAD_DIM)
        rec = _rms(o[:, cs], hnw_ref[:, cs]) * gate[:, cs]
        cat_ref[:, MIX_HALF + hd * HEAD_DIM:MIX_HALF + (hd + 1) * HEAD_DIM] = rec.astype(BF16)
    y = jnp.dot(cat_ref[...], wo_ref[...], preferred_element_type=F32)
    o_ref[0] = x_ref[0] + mod_ref[0, 0][2:3] * y


def _even_merge(px, o_f, o_b, xa, modtab, gnw, ws_bf16, bsb, hnw, wo_bf16, nct):
    b, lt, d = xa.shape
    tm = TOKEN_TILE
    w = MIX_HALF
    tile = lambda width, blk: pl.BlockSpec((1, tm, width), lambda i, t: (i, t, blk))
    const = lambda shape: pl.BlockSpec(shape, lambda i, t: (0,) * len(shape))
    return pl.pallas_call(
        _even_merge_kernel,
        grid=(b, lt // tm),
        in_specs=[
            tile(2 * w, 0), tile(w, 6), tile(w, 0), tile(w, 0), tile(d, 0),
            pl.BlockSpec((1, 1, 8, d), _mod_index(nct)),
            const((1, w)), const((N_HEADS, CHUNK, CHUNK)), const((N_HEADS, CHUNK, HEAD_DIM)),
            const((1, w)), const((2 * w, d)),
        ],
        out_specs=tile(d, 0),
        out_shape=jax.ShapeDtypeStruct((b, lt, d), F32),
        scratch_shapes=[pltpu.VMEM((tm, 2 * w), BF16)],
        input_output_aliases={4: 0},
        compiler_params=_cparams(("parallel", "parallel")),
        name="even_merge",
    )(px, px, o_f, o_b, xa, modtab, gnw, ws_bf16, bsb, hnw, wo_bf16)


def _odd_prep_kernel(p_ref, prev_ref, next_ref, cw_ref, gvec_ref, qnw_ref, wq_ref, qkw_ref, kvnw_ref,
                     wkv_ref, kkw_ref, ct_ref, st_ref,
                     qkv_ref, gates_ref, q_ref, k_ref, v_ref, *, nct, ntiles, scale):
    tm = p_ref.shape[1]
    t = pl.program_id(1)
    w3 = 3 * MIX_HALF
    first = jnp.logical_or(t == 0, t == nct)
    last = jnp.logical_or(t == nct - 1, t == ntiles - 1)
    xin = p_ref[0, :, :w3]
    prev8 = jnp.where(first, 0.0, prev_ref[0])
    next8 = jnp.where(last, 0.0, next_ref[0])
    row8 = lax.broadcasted_iota(jnp.int32, (SUBLANES, w3), 0)
    pad = (GDN_CONV - 1) // 2
    acc = xin * cw_ref[pad:pad + 1, :]
    for sh in range(1, pad + 1):
        dn = pltpu.roll(xin, sh, 0)
        top = jnp.where(row8 < sh, pltpu.roll(prev8, sh, 0), dn[:SUBLANES])
        dn = jnp.concatenate([top, dn[SUBLANES:]], axis=0)
        acc = acc + dn * cw_ref[pad - sh:pad - sh + 1, :]
        up = pltpu.roll(xin, tm - sh, 0)
        bot = jnp.where(row8 >= SUBLANES - sh, pltpu.roll(next8, SUBLANES - sh, 0), up[tm - SUBLANES:])
        up = jnp.concatenate([up[:tm - SUBLANES], bot], axis=0)
        acc = acc + up * cw_ref[pad + sh:pad + sh + 1, :]
    act = _silu(acc)
    for hd in range(N_HEADS):
        for part, mul in ((0, HEAD_DIM ** -0.5), (1, 1.0)):
            cs = slice(part * MIX_HALF + hd * HEAD_DIM, part * MIX_HALF + (hd + 1) * HEAD_DIM)
            a = act[:, cs]
            qkv_ref[0, :, cs] = a * lax.rsqrt(jnp.sum(a * a, axis=-1, keepdims=True) + EPS) * mul
    qkv_ref[0, :, 2 * MIX_HALF:] = act[:, 2 * MIX_HALF:]

    gx = p_ref[0, :, 21 * LANES:22 * LANES]
    lane = lax.broadcasted_iota(jnp.int32, (tm, LANES), 1)
    a_neg = -jnp.exp(gvec_ref[0:1, :])
    gates_ref[0] = jnp.where(lane < 2 * N_HEADS, _sigmoid(gx), a_neg * _softplus(gx + gvec_ref[1:2, :]))

    ct = ct_ref[...]
    st = st_ref[...]
    inv_n = 1.0 / MLA_QK
    cq = _rms(p_ref[0, :, 16 * LANES:18 * LANES], qnw_ref[...])
    qraw = jnp.dot(cq.astype(BF16), wq_ref[...], preferred_element_type=F32)
    hw = MLA_HEADS * LANES
    for hd in range(MLA_HEADS):
        qa = qraw[:, hd * LANES:(hd + 1) * LANES]
        qs = qraw[:, hw + hd * LANES:hw + (hd + 1) * LANES]
        rs = lax.rsqrt(jnp.sum(qa * qa, axis=-1, keepdims=True) * inv_n + EPS)
        rot = (qa * rs * qkw_ref[0:1, :]) * ct + (qs * rs * qkw_ref[1:2, :]) * st
        q_ref[0, hd] = (rot * scale).astype(BF16)
    ckv = _rms(p_ref[0, :, 18 * LANES:19 * LANES], kvnw_ref[...])
    kv = jnp.dot(ckv.astype(BF16), wkv_ref[...], preferred_element_type=F32)
    kr = p_ref[0, :, 19 * LANES:20 * LANES]
    krs = p_ref[0, :, 20 * LANES:21 * LANES]
    for hd in range(MLA_HEADS):
        ka = kv[:, hd * LANES:(hd + 1) * LANES] + kr
        rs = lax.rsqrt(jnp.sum(ka * ka, axis=-1, keepdims=True) * inv_n + EPS)
        rot = (ka * rs * kkw_ref[0:1, :]) * ct + (krs * rs * kkw_ref[1:2, :]) * st
        k_ref[0, hd] = rot.astype(BF16)
        v_ref[0, hd] = jnp.where(lane < MLA_V, kv[:, hw + hd * LANES:hw + (hd + 1) * LANES], 1.0).astype(BF16)


def _odd_prep(px, cw, gvec, qnw, wq, qkw, kvnw, wkv, kkw, ctab, stab, nct):
    b, lt, n = px.shape
    tm = TOKEN_TILE
    ntiles = lt // tm
    w3 = 3 * MIX_HALF
    per = tm // SUBLANES
    nrow8 = lt // SUBLANES
    const = lambda shape: pl.BlockSpec(shape, lambda i, t: (0,) * len(shape))
    head_tile = pl.BlockSpec((1, MLA_HEADS, tm, LANES), lambda i, t: (i, 0, t, 0))
    head_shape = jax.ShapeDtypeStruct((b, MLA_HEADS, lt, LANES), BF16)
    q_tile = pl.BlockSpec((1, MLA_HEADS, tm, LANES), lambda i, t: (i, 0, jnp.maximum(t - nct, 0), 0))
    q_shape = jax.ShapeDtypeStruct((b, MLA_HEADS, lt - nct * tm, LANES), BF16)
    return pl.pallas_call(
        functools.partial(_odd_prep_kernel, nct=nct, ntiles=ntiles, scale=MLA_QK ** -0.5 * math.log2(math.e)),
        grid=(b, ntiles),
        in_specs=[
            pl.BlockSpec((1, tm, n), lambda i, t: (i, t, 0)),
            pl.BlockSpec((1, SUBLANES, w3), lambda i, t: (i, jnp.maximum(t * per - 1, 0), 0)),
            pl.BlockSpec((1, SUBLANES, w3), lambda i, t: (i, jnp.minimum((t + 1) * per, nrow8 - 1), 0)),
            const((8, w3)), const((8, LANES)), const((1, MLA_Q_RANK)),
            const((MLA_Q_RANK, 2 * MLA_HEADS * LANES)), const((8, LANES)), const((1, MLA_KV_RANK)),
            const((MLA_KV_RANK, 2 * MLA_HEADS * LANES)), const((8, LANES)),
            pl.BlockSpec((tm, LANES), lambda i, t: (t, 0)),
            pl.BlockSpec((tm, LANES), lambda i, t: (t, 0)),
        ],
        out_specs=[
            pl.BlockSpec((1, tm, w3), lambda i, t: (i, t, 0)),
            pl.BlockSpec((1, tm, LANES), lambda i, t: (i, t, 0)),
            q_tile, head_tile, head_tile,
        ],
        out_shape=[
            jax.ShapeDtypeStruct((b, lt, w3), F32),
            jax.ShapeDtypeStruct((b, lt, LANES), F32),
            q_shape, head_shape, head_shape,
        ],
        compiler_params=_cparams(("parallel", "arbitrary")),
        name="odd_prep",
    )(px, px, px, cw, gvec, qnw, wq, qkw, kvnw, wkv, kkw, ctab, stab)


def _gdn_kernel(qf_ref, kf_ref, vf_ref, gf_ref, qb_ref, kb_ref, vb_ref, gb_ref, of_ref, ob_ref, s_ref):
    @pl.when(pl.program_id(1) == 0)
    def _():
        s_ref[...] = jnp.zeros_like(s_ref)

    c = CHUNK
    nb = 2 * N_HEADS
    heads = lambda ref: [ref[0, :, hd * HEAD_DIM:(hd + 1) * HEAD_DIM] for hd in range(N_HEADS)]
    q = jnp.stack(heads(qf_ref) + heads(qb_ref))
    k = jnp.stack(heads(kf_ref) + heads(kb_ref))
    v = jnp.stack(heads(vf_ref) + heads(vb_ref))
    gf = gf_ref[0]
    gb = gb_ref[0]
    lane_col = lambda gt, j: jnp.broadcast_to(gt[:, j:j + 1], (c, LANES))
    beta = jnp.stack([lane_col(gf, hd) for hd in range(N_HEADS)]
                     + [lane_col(gb, N_HEADS + hd) for hd in range(N_HEADS)])
    la = jnp.stack([lane_col(gf, 2 * N_HEADS + hd) for hd in range(N_HEADS)]
                   + [lane_col(gb, 3 * N_HEADS + hd) for hd in range(N_HEADS)])

    bi = lax.broadcasted_iota(jnp.int32, (nb, c, c), 0)
    ri = lax.broadcasted_iota(jnp.int32, (nb, c, c), 1)
    ci = lax.broadcasted_iota(jnp.int32, (nb, c, c), 2)
    rev = bi >= N_HEADS
    fwd = jnp.logical_not(rev)
    incl = jnp.logical_or(jnp.logical_and(rev, ci >= ri), jnp.logical_and(fwd, ci <= ri))
    strict = jnp.logical_and(incl, ci != ri)
    tri = jnp.where(incl, 1.0, 0.0).astype(BF16)

    g = None
    for part in _split_bf16(la, 3):
        t = lax.dot_general(tri, part, BMM_DIMS, preferred_element_type=F32)
        g = t if g is None else g + t
    gamma = jnp.exp(jnp.where(incl, g - jnp.swapaxes(g, 1, 2), -jnp.inf))
    kq = _bmm(jnp.concatenate([k, q], axis=1), k, BMM_NT_DIMS)
    nm = jnp.where(strict, beta * kq[:, :c] * gamma, 0.0)

    x = None
    for lv in range(int(math.log2(c))):
        same = (ri >> (lv + 1)) == (ci >> (lv + 1))
        r_up = ((ri >> lv) & 1) == 1
        c_up = ((ci >> lv) & 1) == 1
        pick = jnp.logical_and(jnp.logical_xor(r_up, c_up), jnp.logical_xor(r_up, rev))
        blk = jnp.where(jnp.logical_and(same, pick), nm, 0.0)
        if x is None:
            x = jnp.where(ri == ci, 1.0, 0.0) - blk
        else:
            x = x - _bmm(x, _bmm(blk, x))

    eg = jnp.exp(g)
    uw = _bmm(x, jnp.concatenate([beta * v, beta * k * eg], axis=2))
    qk = jnp.where(incl, kq[:, c:] * gamma, 0.0)
    g_tot = jnp.concatenate([g[:N_HEADS, c - 1:c, :], g[N_HEADS:, 0:1, :]], axis=0)
    s = s_ref[...]
    v_new = uw[:, :, :HEAD_DIM] - _bmm(uw[:, :, HEAD_DIM:], s)
    o = _bmm(q * eg, s) + _bmm(qk, v_new)
    s_ref[...] = jnp.exp(g_tot) * s + _bmm(k * jnp.exp(g_tot - g), v_new, BMM_TN_DIMS)
    for hd in range(N_HEADS):
        of_ref[0, :, hd * HEAD_DIM:(hd + 1) * HEAD_DIM] = o[hd]
        ob_ref[0, :, hd * HEAD_DIM:(hd + 1) * HEAD_DIM] = o[N_HEADS + hd]


def _gdn_scan(qkv, gates, nc_ctx):
    b, lt, _ = qkv.shape
    nc = lt // CHUNK
    w = MIX_HALF
    specs = []
    for rev in (False, True):
        order = _chunk_order(nc_ctx, nc, rev)
        specs += [pl.BlockSpec((1, CHUNK, w), lambda i, s, o=order, blk=blk: (i, o(s), blk)) for blk in range(3)]
        specs += [pl.BlockSpec((1, CHUNK, LANES), lambda i, s, o=order: (i, o(s), 0))]
    order_b = _chunk_order(nc_ctx, nc, True)
    return pl.pallas_call(
        _gdn_kernel,
        grid=(b, nc),
        in_specs=specs,
        out_specs=[pl.BlockSpec((1, CHUNK, w), lambda i, s: (i, s, 0)),
                   pl.BlockSpec((1, CHUNK, w), lambda i, s: (i, order_b(s), 0))],
        out_shape=[jax.ShapeDtypeStruct((b, lt, w), F32), jax.ShapeDtypeStruct((b, lt, w), F32)],
        scratch_shapes=[pltpu.VMEM((2 * N_HEADS, HEAD_DIM, HEAD_DIM), F32)],
        compiler_params=_cparams(("parallel", "arbitrary")),
        name="gdn_scan",
    )(qkv, qkv, qkv, gates, qkv, qkv, qkv, gates)


def _attn_kernel(q_ref, k_ref, v_ref, o_ref, m_ref, acc_ref, s_ref, *, tk, nk):
    m_ref[...] = jnp.full_like(m_ref, -jnp.inf)
    acc_ref[...] = jnp.zeros_like(acc_ref)

    def tile(j):
        return pl.ds(pl.multiple_of(j * tk, tk), tk)

    def scores(j, slot):
        for a in range(2):
            s_ref[slot, a] = lax.dot_general(q_ref[0, a], k_ref[0, a, tile(j), :], NT_DIMS,
                                             preferred_element_type=F32)

    def softmax_pv(j, slot):
        for a in range(2):
            s = s_ref[slot, a]
            m_prev = m_ref[a]
            m_new = jnp.maximum(m_prev, jnp.max(s, axis=-1, keepdims=True))
            p = jnp.exp2(s - m_new[:, :1]).astype(BF16)
            acc_ref[a] = jnp.exp2(m_prev - m_new) * acc_ref[a] + jnp.dot(
                p, v_ref[0, a, tile(j), :], preferred_element_type=F32)
            m_ref[a] = m_new

    scores(0, 0)

    def pair(i, carry):
        scores(2 * i + 1, 1)
        softmax_pv(2 * i, 0)
        scores(jnp.minimum(2 * i + 2, nk - 1), 0)
        softmax_pv(2 * i + 1, 1)
        return carry

    lax.fori_loop(0, nk // 2, pair, 0)
    if nk % 2:
        softmax_pv(nk - 1, 0)
    lane = lax.broadcasted_iota(jnp.int32, acc_ref.shape[1:], 1)
    a0 = acc_ref[0]
    a1 = acc_ref[1]
    o_ref[0] = jnp.where(lane < MLA_V, a0 / pltpu.roll(a0, MLA_V, 1), pltpu.roll(a1, MLA_V, 1) / a1)


def _attention(q, k, v):
    b, nh, lq, _ = q.shape
    lt = k.shape[2]
    tq = next(c for c in ATTN_TQ_CANDIDATES if lq % c == 0)
    tk = next(c for c in ATTN_TK_CANDIDATES if lt % c == 0)
    return pl.pallas_call(
        functools.partial(_attn_kernel, tk=tk, nk=lt // tk),
        grid=(b, nh // 2, lq // tq),
        in_specs=[
            pl.BlockSpec((1, 2, tq, LANES), lambda i, h, qi: (i, h, qi, 0)),
            pl.BlockSpec((1, 2, lt, LANES), lambda i, h, qi: (i, h, 0, 0)),
            pl.BlockSpec((1, 2, lt, LANES), lambda i, h, qi: (i, h, 0, 0)),
        ],
        out_specs=pl.BlockSpec((1, tq, 2 * MLA_V), lambda i, h, qi: (i, qi, h)),
        out_shape=jax.ShapeDtypeStruct((b, lq, nh * MLA_V), F32),
        scratch_shapes=[pltpu.VMEM((2, tq, LANES), F32), pltpu.VMEM((2, tq, LANES), F32),
                        pltpu.VMEM((2, 2, tq, tk), F32)],
        compiler_params=_cparams(("parallel", "parallel", "arbitrary")),
        name="mla_attention",
    )(q, k, v)


def _odd_merge_kernel(z_ref, of_ref, ob_ref, att_ref, x_ref, mod_ref, gnw_ref, wo_ref, o_ref, cat_ref):
    o = of_ref[0] + ob_ref[0]
    gate = _silu(z_ref[0])
    for hd in range(N_HEADS):
        cs = slice(hd * HEAD_DIM, (hd + 1) * HEAD_DIM)
        cat_ref[:, cs] = (_rms(o[:, cs], gnw_ref[...]) * gate[:, cs]).astype(BF16)
    cat_ref[:, MIX_HALF:] = att_ref[0].astype(BF16)
    y = jnp.dot(cat_ref[...], wo_ref[...], preferred_element_type=F32)
    o_ref[0] = x_ref[0] + mod_ref[0, 0][2:3] * y


def _odd_merge(px, o_f, o_b, att, xa, modtab, gnw, wo_bf16, nct):
    b, lt, d = xa.shape
    tm = TOKEN_TILE
    lq = att.shape[1]
    w = MIX_HALF
    lat = lambda width, blk: pl.BlockSpec((1, tm, width), lambda i, t: (i, t + nct, blk))
    const = lambda shape: pl.BlockSpec(shape, lambda i, t: (0,) * len(shape))
    return pl.pallas_call(
        _odd_merge_kernel,
        grid=(b, lq // tm),
        in_specs=[
            lat(w, 3), lat(w, 0), lat(w, 0),
            pl.BlockSpec((1, tm, w), lambda i, t: (i, t, 0)),
            lat(d, 0),
            pl.BlockSpec((1, 1, 8, d), lambda i, t: (i, 1, 0, 0)),
            const((1, HEAD_DIM)), const((2 * w, d)),
        ],
        out_specs=lat(d, 0),
        out_shape=jax.ShapeDtypeStruct((b, lt, d), F32),
        scratch_shapes=[pltpu.VMEM((tm, 2 * w), BF16)],
        input_output_aliases={4: 0},
        compiler_params=_cparams(("parallel", "parallel")),
        name="odd_merge",
    )(px, o_f, o_b, att, xa, modtab, gnw, wo_bf16)


def _router_kernel(x_ref, mod_ref, nw_ref, wr_ref, br_ref, h_ref, info_ref, cnt_ref, run_ref):
    @pl.when(jnp.logical_and(pl.program_id(0) == 0, pl.program_id(1) == 0))
    def _():
        run_ref[...] = jnp.zeros_like(run_ref)

    tm = x_ref.shape[1]
    mod = mod_ref[0, 0]
    h = _rms(x_ref[0], nw_ref[...]) * (1.0 + mod[4:5]) + mod[3:4]
    h_ref[0] = h
    logits = _dot3(h, wr_ref[...]) + br_ref[...]
    lane = lax.broadcasted_iota(jnp.int32, (tm, LANES), 1)
    lane_f = lane.astype(F32)
    big = float(LANES)
    neg = -jnp.inf
    is_g = jnp.logical_and(lane >= N_EXPERTS, lane < N_EXPERTS + N_GROUPS)
    lg = jnp.where(is_g, logits, neg)
    mg = jnp.max(lg, axis=-1, keepdims=True)
    pg_top = 1.0 / jnp.sum(jnp.exp(lg - mg), axis=-1, keepdims=True)
    g_idx = jnp.min(jnp.where(lg == mg, lane_f - N_EXPERTS, big), axis=-1, keepdims=True)
    in_g = jnp.logical_and(lane < N_EXPERTS, (lane >> 3).astype(F32) == g_idx)
    le = jnp.where(in_g, logits, neg)
    m1 = jnp.max(le, axis=-1, keepdims=True)
    e1 = jnp.min(jnp.where(le == m1, lane_f, big), axis=-1, keepdims=True)
    le2 = jnp.where(lane_f == e1, neg, le)
    m2 = jnp.max(le2, axis=-1, keepdims=True)
    e2 = jnp.min(jnp.where(le2 == m2, lane_f, big), axis=-1, keepdims=True)
    se = jnp.sum(jnp.exp(le - m1), axis=-1, keepdims=True)
    p1 = 1.0 / se
    p2 = jnp.exp(m2 - m1) / se
    w1 = pg_top * p1 / (p1 + p2)
    w2 = pg_top * p2 / (p1 + p2)
    hit1 = lane_f == e1
    hit2 = lane_f == e2
    oh = jnp.where(jnp.logical_or(hit1, hit2), 1.0, 0.0)
    r = lax.broadcasted_iota(jnp.int32, (tm, tm), 0)
    c = lax.broadcasted_iota(jnp.int32, (tm, tm), 1)
    before = jnp.dot(jnp.where(c < r, 1.0, 0.0).astype(BF16), oh.astype(BF16), preferred_element_type=F32)
    base = before + run_ref[0:1, :]
    pos1 = jnp.sum(jnp.where(hit1, base, 0.0), axis=-1, keepdims=True)
    pos2 = jnp.sum(jnp.where(hit2, base, 0.0), axis=-1, keepdims=True)
    run = run_ref[0:1, :] + jnp.sum(oh, axis=0, keepdims=True)
    run_ref[...] = jnp.broadcast_to(run, run_ref.shape)
    cnt_ref[...] = jnp.broadcast_to(run, cnt_ref.shape)
    info = jnp.zeros((tm, LANES), F32)
    for j, val in enumerate((e1, e2, pos1, pos2, w1, w2)):
        info = jnp.where(lane == j, val, info)
    info_ref[0] = info


def _router(xa, modtab, norm_w, wr, br, nct):
    b, lt, d = xa.shape
    tm = TOKEN_TILE
    const = lambda shape: pl.BlockSpec(shape, lambda i, t: (0,) * len(shape))
    return pl.pallas_call(
        _router_kernel,
        grid=(b, lt // tm),
        in_specs=[
            pl.BlockSpec((1, tm, d), lambda i, t: (i, t, 0)),
            pl.BlockSpec((1, 1, 8, d), _mod_index(nct)),
            const((1, d)), const((d, LANES)), const((1, LANES)),
        ],
        out_specs=[
            pl.BlockSpec((1, tm, d), lambda i, t: (i, t, 0)),
            pl.BlockSpec((1, tm, LANES), lambda i, t: (i, t, 0)),
            const((SUBLANES, LANES)),
        ],
        out_shape=[
            jax.ShapeDtypeStruct((b, lt, d), F32),
            jax.ShapeDtypeStruct((b, lt, LANES), F32),
            jax.ShapeDtypeStruct((SUBLANES, LANES), F32),
        ],
        scratch_shapes=[pltpu.VMEM((SUBLANES, LANES), F32)],
        compiler_params=_cparams(("arbitrary", "arbitrary")),
        name="moe_router",
    )(xa, modtab, norm_w.reshape(1, d), wr, br)


def _dispatch_kernel(dest_ref, h_ref, xs_in_ref, xs_ref, sem):
    del xs_in_ref
    tm = h_ref.shape[0]

    def row_copy(r, k):
        return pltpu.make_async_copy(h_ref.at[pl.ds(r, 1)], xs_ref.at[pl.ds(dest_ref[0, 0, 2 * r + k], 1)], sem)

    def issue(r, carry):
        row_copy(r, 0).start(priority=0)
        row_copy(r, 1).start(priority=1)
        return carry

    def drain(r, carry):
        row_copy(r, 0).wait()
        row_copy(r, 1).wait()
        return carry

    lax.fori_loop(0, tm, issue, 0, unroll=ROW_DMA_UNROLL)
    lax.fori_loop(0, tm, drain, 0, unroll=ROW_DMA_UNROLL)


def _dispatch(dest3, h2d, n_slots):
    t, d = h2d.shape
    tm = TOKEN_TILE
    xs0 = jnp.zeros((n_slots, d), F32)
    return pl.pallas_call(
        _dispatch_kernel,
        grid=(t // tm,),
        in_specs=[
            pl.BlockSpec((1, 1, 2 * tm), lambda i: (i, 0, 0), memory_space=pltpu.SMEM),
            pl.BlockSpec((tm, d), lambda i: (i, 0)),
            pl.BlockSpec(memory_space=pl.ANY),
        ],
        out_specs=pl.BlockSpec(memory_space=pl.ANY),
        out_shape=jax.ShapeDtypeStruct((n_slots, d), F32),
        scratch_shapes=[pltpu.SemaphoreType.DMA],
        input_output_aliases={2: 0},
        compiler_params=_cparams(("arbitrary",)),
        name="moe_dispatch",
    )(dest3, h2d, xs0)


def _expert_kernel(be_ref, nu_ref, xs_ref, wg_ref, wu_ref, wd_ref, ys_ref):
    i = pl.program_id(0)

    @pl.when(i < nu_ref[0])
    def _():
        x = xs_ref[...].astype(BF16)
        a = jnp.dot(x, wg_ref[0].astype(BF16), preferred_element_type=F32)
        u = jnp.dot(x, wu_ref[0].astype(BF16), preferred_element_type=F32)
        ys_ref[...] = jnp.dot((_silu(a) * u).astype(BF16), wd_ref[0].astype(BF16), preferred_element_type=F32)

    @pl.when(i >= nu_ref[0])
    def _():
        ys_ref[...] = jnp.zeros_like(ys_ref)


def _experts(blk_e, n_used, xs, wg, wu, wd):
    n_slots, d = xs.shape
    ff = wg.shape[2]
    nb = n_slots // MOE_BLOCK
    grid_spec = pltpu.PrefetchScalarGridSpec(
        num_scalar_prefetch=2,
        grid=(nb,),
        in_specs=[
            pl.BlockSpec((MOE_BLOCK, d), lambda i, be, nu: (jnp.minimum(i, jnp.maximum(nu[0] - 1, 0)), 0)),
            pl.BlockSpec((1, d, ff), lambda i, be, nu: (be[i], 0, 0)),
            pl.BlockSpec((1, d, ff), lambda i, be, nu: (be[i], 0, 0)),
            pl.BlockSpec((1, ff, d), lambda i, be, nu: (be[i], 0, 0)),
        ],
        out_specs=pl.BlockSpec((MOE_BLOCK, d), lambda i, be, nu: (i, 0)),
    )
    return pl.pallas_call(
        _expert_kernel,
        grid_spec=grid_spec,
        out_shape=jax.ShapeDtypeStruct((n_slots, d), F32),
        compiler_params=_cparams(("arbitrary",)),
        name="moe_experts",
    )(blk_e, n_used, xs, wg, wu, wd)


def _combine_kernel(dest_ref, info_ref, x_ref, mod_ref, ys_ref, o_ref, buf_ref, sem):
    tm = x_ref.shape[1]

    def row_copy(r, k):
        return pltpu.make_async_copy(ys_ref.at[pl.ds(dest_ref[0, 0, 2 * r + k], 1)],
                                     buf_ref.at[k, pl.ds(r, 1)], sem)

    def issue(r, carry):
        row_copy(r, 0).start(priority=0)
        row_copy(r, 1).start(priority=1)
        return carry

    def drain(r, carry):
        row_copy(r, 0).wait()
        row_copy(r, 1).wait()
        return carry

    lax.fori_loop(0, tm, issue, 0, unroll=ROW_DMA_UNROLL)
    lax.fori_loop(0, tm, drain, 0, unroll=ROW_DMA_UNROLL)
    info = info_ref[0]
    y = info[:, 4:5] * buf_ref[0] + info[:, 5:6] * buf_ref[1]
    o_ref[0] = x_ref[0] + mod_ref[0, 0][5:6] * y


def _combine(dest3, info, xa, modtab, ys, nct, latent_only):
    b, lt, d = xa.shape
    tm = TOKEN_TILE
    nt = lt // tm
    skip = nct if latent_only else 0
    mod_map = _mod_index(nct)
    return pl.pallas_call(
        _combine_kernel,
        grid=(b, nt - skip),
        in_specs=[
            pl.BlockSpec((1, 1, 2 * tm), lambda i, t: (i * nt + t + skip, 0, 0), memory_space=pltpu.SMEM),
            pl.BlockSpec((1, tm, LANES), lambda i, t: (i, t + skip, 0)),
            pl.BlockSpec((1, tm, d), lambda i, t: (i, t + skip, 0)),
            pl.BlockSpec((1, 1, 8, d), lambda i, t: mod_map(i, t + skip)),
            pl.BlockSpec(memory_space=pl.ANY),
        ],
        out_specs=pl.BlockSpec((1, tm, d), lambda i, t: (i, t, 0)),
        out_shape=jax.ShapeDtypeStruct((b, lt - skip * tm, d), F32),
        scratch_shapes=[pltpu.VMEM((2, tm, d), F32), pltpu.SemaphoreType.DMA],
        input_output_aliases={} if latent_only else {2: 0},
        compiler_params=_cparams(("arbitrary", "arbitrary")),
        name="moe_combine",
    )(dest3, info, xa, modtab, ys)


def _moe(xa, modtab, norm_w, wg_r, bg_r, we_r, be_r, w_gate, w_up, w_down, nct, latent_only=False):
    b, lt, d = xa.shape
    t = b * lt
    tm = TOKEN_TILE
    pad = LANES - N_EXPERTS - N_GROUPS
    wr = jnp.concatenate([we_r, wg_r, jnp.zeros((d, pad), F32)], axis=1)
    br = jnp.concatenate([be_r, bg_r, jnp.zeros((pad,), F32)]).reshape(1, LANES)
    h, info, cnt = _router(xa, modtab, norm_w, wr, br, nct)
    counts = cnt[0, :N_EXPERTS].astype(jnp.int32)
    padded = (counts + MOE_BLOCK - 1) // MOE_BLOCK * MOE_BLOCK
    pend = jnp.cumsum(padded)
    pstart = pend - padded
    info2 = info.reshape(t, LANES)
    e_idx = info2[:, 0:2].astype(jnp.int32)
    first_slot = jnp.sum(jnp.where(e_idx[:, :, None] == jnp.arange(N_EXPERTS, dtype=jnp.int32), pstart, 0), axis=-1)
    dest = first_slot + info2[:, 2:4].astype(jnp.int32)
    dest3 = dest.reshape(t // tm, 1, 2 * tm)
    n_blk = -(-2 * t // MOE_BLOCK) + N_EXPERTS
    blk_start = jnp.arange(n_blk, dtype=jnp.int32) * MOE_BLOCK
    blk_e = jnp.minimum(jnp.sum((pend[None, :] <= blk_start[:, None]).astype(jnp.int32), axis=1), N_EXPERTS - 1)
    n_used = (pend[-1:] // MOE_BLOCK).astype(jnp.int32)
    xs = _dispatch(dest3, h.reshape(t, d), n_blk * MOE_BLOCK)
    ys = _experts(blk_e, n_used, xs, w_gate, w_up, w_down)
    return _combine(dest3, info, xa, modtab, ys, nct, latent_only)


_ROPE_SWAP = np.concatenate([np.arange(8, 16), np.arange(0, 8), np.arange(24, 32), np.arange(16, 24)])


def _odd_in_weights(w_in):
    d = w_in.shape[0]
    w4 = 4 * MIX_HALF
    o_beta = w4
    o_cq = o_beta + 4 * N_HEADS
    o_ckv = o_cq + MLA_Q_RANK
    o_kr = o_ckv + MLA_KV_RANK
    z = lambda n: jnp.zeros((d, n), F32)
    kr = w_in[:, o_kr:o_kr + MLA_ROPE]
    blocks = [
        w_in[:, :w4], w_in[:, o_cq:o_ckv], w_in[:, o_ckv:o_kr],
        z(MLA_NOPE), kr, z(LANES - MLA_QK),
        z(MLA_NOPE), kr[:, _ROPE_SWAP], z(LANES - MLA_QK),
        w_in[:, o_beta:o_cq], z(LANES - 4 * N_HEADS),
    ]
    return jnp.concatenate(blocks, axis=1)


def _per_head_lanes(w, per, take, width):
    r = w.shape[0]
    wh = w.reshape(r, -1, per)[:, :, take]
    return jnp.pad(wh, ((0, 0), (0, 0), (0, width - wh.shape[2]))).reshape(r, -1)


def _rope_lane_weights(w):
    plain = jnp.pad(w, (0, LANES - MLA_QK))
    sw = jnp.pad(jnp.concatenate([jnp.zeros((MLA_NOPE,), F32), w[MLA_NOPE:][_ROPE_SWAP]]), (0, LANES - MLA_QK))
    return jnp.pad(jnp.stack([plain, sw]), ((0, 6), (0, 0)))


def _rope_tables(n_ctx, n_lat):
    quarter = MLA_ROPE // 4
    freqs = ROPE_THETA ** (-jnp.arange(quarter, dtype=F32) / quarter)
    pos = jnp.arange(n_lat)
    row = (pos // GRID_W).astype(F32)[:, None] * freqs
    col = (pos % GRID_W).astype(F32)[:, None] * freqs
    ones = jnp.ones((n_lat, MLA_NOPE), F32)
    zpad = jnp.zeros((n_lat, LANES - MLA_QK), F32)
    ctab = jnp.concatenate([ones, jnp.cos(row), jnp.cos(row), jnp.cos(col), jnp.cos(col), zpad], axis=1)
    stab = jnp.concatenate([0 * ones, -jnp.sin(row), jnp.sin(row), -jnp.sin(col), jnp.sin(col), zpad], axis=1)
    c_ctx = jnp.concatenate([jnp.ones((n_ctx, MLA_QK), F32), jnp.zeros((n_ctx, LANES - MLA_QK), F32)], axis=1)
    return jnp.concatenate([c_ctx, ctab], axis=0), jnp.concatenate([jnp.zeros((n_ctx, LANES), F32), stab], axis=0)


def _lower_bound(logits, layer):
    return jnp.cumsum(jax.nn.softmax(logits.astype(F32), axis=0), axis=0)[layer]


def kernel(x, c, ctx, c_ctx, ada_w, ada_b, norm_mix_w, norm_ffn_w, even_w_in, even_w_out, gmlp_norm_w, gmlp_ws, gmlp_bs, hgrn_lb_logits, hgrn_norm_w, odd_w_in, odd_w_out, gdn_conv_w, gdn_a_log, gdn_dt_bias, gdn_norm_w, mla_q_norm_w, mla_wq_up, mla_kv_norm_w, mla_wkv_up, mla_qk_norm_q, mla_qk_norm_k, router_group_w, router_group_b, router_expert_w, router_expert_b, moe_w_gate, moe_w_up, moe_w_down):
    b, n_lat, d = x.shape
    n_ctx = ctx.shape[1]
    depth = ada_w.shape[0]
    tm = TOKEN_TILE
    assert depth == 2 and n_ctx % tm == 0 and n_lat % tm == 0 and n_lat % GRID_W == 0
    nct = n_ctx // tm
    ncc = n_ctx // CHUNK

    xa = jnp.concatenate([ctx, x], axis=1)
    rows = -(-(b + 1) // SUBLANES) * SUBLANES
    cvec = jnp.zeros((rows, d), F32).at[:b].set(c).at[b].set(c_ctx)
    mods = _mod_vectors(cvec, ada_w, ada_b).reshape(depth, rows, 6, d)

    def mod_table(layer):
        mx = mods[layer, :b]
        mc = jnp.broadcast_to(mods[layer, b], (b, 6, d))
        return jnp.pad(jnp.stack([mc, mx], axis=1), ((0, 0), (0, 0), (0, 2), (0, 0)))

    modtab = mod_table(0)
    px = _in_projection(xa, modtab, norm_mix_w[0], even_w_in[0].astype(BF16), nct)
    lb_f = _lower_bound(hgrn_lb_logits[0], 0).reshape(1, MIX_HALF)
    lb_b = _lower_bound(hgrn_lb_logits[1], 0).reshape(1, MIX_HALF)
    o_f, o_b = _hgrn_scan(px, lb_f, lb_b, ncc)
    bsb = jnp.broadcast_to(gmlp_bs[0][:, :, None], (N_HEADS, CHUNK, HEAD_DIM))
    xa = _even_merge(px, o_f, o_b, xa, modtab, gmlp_norm_w[0].reshape(1, MIX_HALF), gmlp_ws[0].astype(BF16), bsb,
                     hgrn_norm_w[0].reshape(1, MIX_HALF), even_w_out[0].astype(BF16), nct)
    xa = _moe(xa, modtab, norm_ffn_w[0], router_group_w[0], router_group_b[0], router_expert_w[0],
              router_expert_b[0], moe_w_gate[0], moe_w_up[0], moe_w_down[0], nct)

    modtab = mod_table(1)
    px = _in_projection(xa, modtab, norm_mix_w[1], _odd_in_weights(odd_w_in[0]).astype(BF16), nct)
    cw = jnp.pad(gdn_conv_w[0], ((0, 8 - GDN_CONV), (0, 0)))
    gvec = jnp.zeros((8, LANES), F32)
    gvec = gvec.at[0, 2 * N_HEADS:4 * N_HEADS].set(gdn_a_log[0].reshape(-1))
    gvec = gvec.at[1, 2 * N_HEADS:4 * N_HEADS].set(gdn_dt_bias[0].reshape(-1))
    wq = mla_wq_up[0]
    rope_cols = MLA_NOPE + _ROPE_SWAP
    wq_all = jnp.concatenate([
        _per_head_lanes(wq, MLA_QK, np.arange(MLA_QK), LANES),
        jnp.pad(_per_head_lanes(wq, MLA_QK, rope_cols, LANES - MLA_NOPE).reshape(MLA_Q_RANK, MLA_HEADS, -1),
                ((0, 0), (0, 0), (MLA_NOPE, 0)))[:, :, :LANES].reshape(MLA_Q_RANK, -1),
    ], axis=1).astype(BF16)
    wkv = mla_wkv_up[0]
    wkv_all = jnp.concatenate([
        _per_head_lanes(wkv, MLA_NOPE + MLA_V, np.arange(MLA_NOPE), LANES),
        _per_head_lanes(wkv, MLA_NOPE + MLA_V, MLA_NOPE + np.arange(MLA_V), LANES),
    ], axis=1).astype(BF16)
    ctab, stab = _rope_tables(n_ctx, n_lat)
    qkv, gates, q, k, v = _odd_prep(px, cw, gvec, mla_q_norm_w[0].reshape(1, -1), wq_all,
                                    _rope_lane_weights(mla_qk_norm_q[0]), mla_kv_norm_w[0].reshape(1, -1), wkv_all,
                                    _rope_lane_weights(mla_qk_norm_k[0]), ctab, stab, nct)
    o_f, o_b = _gdn_scan(qkv, gates, ncc)
    att = _attention(q, k, v)
    xa = _odd_merge(px, o_f, o_b, att, xa, modtab, gdn_norm_w[0].reshape(1, HEAD_DIM), odd_w_out[0].astype(BF16), nct)
    return _moe(xa, modtab, norm_ffn_w[1], router_group_w[1], router_group_b[1], router_expert_w[1],
                router_expert_b[1], moe_w_gate[1], moe_w_up[1], moe_w_down[1], nct, latent_only=True)
```

```python
import functools
import math

import jax
import jax.numpy as jnp
import numpy as np
from jax import lax
from jax.experimental import pallas as pl
from jax.experimental.pallas import tpu as pltpu

F32 = jnp.float32
BF16 = jnp.bfloat16
EPS = 1e-6

LANES = 128
SUBLANES = 8
TOKEN_TILE = 256
CHUNK = 128
VMEM_LIMIT = 56 * 1024 * 1024

GRID_W = 64
ROPE_THETA = 10000.0
HEAD_DIM = 128
N_HEADS = 4
MIX_HALF = N_HEADS * HEAD_DIM
GDN_CONV = 5
MLA_HEADS = 8
MLA_NOPE = 64
MLA_ROPE = 32
MLA_V = 64
MLA_QK = MLA_NOPE + MLA_ROPE
MLA_Q_RANK = 256
MLA_KV_RANK = 128
N_GROUPS = 4
EXPERTS_PER_GROUP = 8
N_EXPERTS = N_GROUPS * EXPERTS_PER_GROUP
MOE_BLOCK = 256
ROW_DMA_UNROLL = 2
ATTN_TQ_CANDIDATES = (1024, 512, 256)
ATTN_TK_CANDIDATES = (768, 512, 384, 256, 128)

NT_DIMS = (((1,), (1,)), ((), ()))
TN_DIMS = (((0,), (0,)), ((), ()))


def _cparams(sem):
    return pltpu.CompilerParams(dimension_semantics=sem, vmem_limit_bytes=VMEM_LIMIT)


def _sigmoid(x):
    return 1.0 / (1.0 + jnp.exp(-x))


def _silu(x):
    return x * _sigmoid(x)


def _gelu_tanh(x):
    c = math.sqrt(2.0 / math.pi)
    return 0.5 * x * (1.0 + jnp.tanh(c * (x + 0.044715 * (x * x * x))))


def _softplus(x):
    return jnp.maximum(x, 0.0) + jnp.log(1.0 + jnp.exp(-jnp.abs(x)))


def _bdot(a, b):
    return jnp.dot(a.astype(BF16), b.astype(BF16), preferred_element_type=F32)


def _bdot_nt(a, b):
    return lax.dot_general(a.astype(BF16), b.astype(BF16), NT_DIMS, preferred_element_type=F32)


def _bdot_tn(a, b):
    return lax.dot_general(a.astype(BF16), b.astype(BF16), TN_DIMS, preferred_element_type=F32)


BMM_DIMS = (((2,), (1,)), ((0,), (0,)))
BMM_NT_DIMS = (((2,), (2,)), ((0,), (0,)))
BMM_TN_DIMS = (((1,), (1,)), ((0,), (0,)))


def _bmm(a, b, dims=BMM_DIMS):
    return lax.dot_general(a.astype(BF16), b.astype(BF16), dims, preferred_element_type=F32)


def _split_bf16(x, n):
    parts = []
    r = x
    for _ in range(n):
        p = r.astype(BF16)
        parts.append(p)
        r = r - p.astype(F32)
    return parts


def _dot_exact_lhs(m01, x):
    acc = None
    for p in _split_bf16(x, 3):
        t = jnp.dot(m01, p, preferred_element_type=F32)
        acc = t if acc is None else acc + t
    return acc


def _dot3(a, b):
    a_hi, a_lo = _split_bf16(a, 2)
    b_hi, b_lo = _split_bf16(b, 2)
    d = lambda u, v: jnp.dot(u, v, preferred_element_type=F32)
    return d(a_hi, b_hi) + (d(a_hi, b_lo) + d(a_lo, b_hi))


def _rms(x, w, n=None):
    n = x.shape[-1] if n is None else n
    ms = jnp.sum(x * x, axis=-1, keepdims=True) * (1.0 / n)
    return x * lax.rsqrt(ms + EPS) * w


def _mod_kernel(c_ref, w_ref, b_ref, o_ref):
    s = _silu(c_ref[...])
    o_ref[0] = _dot3(s, w_ref[0]) + b_ref[0]


def _mod_vectors(cvec, ada_w, ada_b):
    depth, d, n = ada_w.shape
    rows = cvec.shape[0]
    tn = 512
    return pl.pallas_call(
        _mod_kernel,
        grid=(depth, n // tn),
        in_specs=[
            pl.BlockSpec((rows, d), lambda l, j: (0, 0)),
            pl.BlockSpec((1, d, tn), lambda l, j: (l, 0, j)),
            pl.BlockSpec((1, 1, tn), lambda l, j: (l, 0, j)),
        ],
        out_specs=pl.BlockSpec((1, rows, tn), lambda l, j: (l, 0, j)),
        out_shape=jax.ShapeDtypeStruct((depth, rows, n), F32),
        compiler_params=_cparams(("parallel", "parallel")),
        name="adaln_mod",
    )(cvec, ada_w, ada_b.reshape(depth, 1, n))


def _mod_index(nct):
    return lambda b, t: (b, jnp.where(t < nct, 0, 1), 0, 0)


def _inproj_kernel(x_ref, mod_ref, nw_ref, w_ref, o_ref, *, shift_row, tn):
    mod = mod_ref[0, 0]
    h = _rms(x_ref[0], nw_ref[...]) * (1.0 + mod[shift_row + 1:shift_row + 2]) + mod[shift_row:shift_row + 1]
    hb = h.astype(BF16)
    for j in range(w_ref.shape[1] // tn):
        o_ref[0, :, j * tn:(j + 1) * tn] = jnp.dot(hb, w_ref[:, j * tn:(j + 1) * tn], preferred_element_type=F32)


def _in_projection(xa, modtab, norm_w, w_bf16, nct):
    b, lt, d = xa.shape
    n = w_bf16.shape[1]
    tm = TOKEN_TILE
    tn = 512 if n % 512 == 0 else 256
    return pl.pallas_call(
        functools.partial(_inproj_kernel, shift_row=0, tn=tn),
        grid=(b, lt // tm),
        in_specs=[
            pl.BlockSpec((1, tm, d), lambda i, t: (i, t, 0)),
            pl.BlockSpec((1, 1, 8, d), _mod_index(nct)),
            pl.BlockSpec((1, d), lambda i, t: (0, 0)),
            pl.BlockSpec((d, n), lambda i, t: (0, 0)),
        ],
        out_specs=pl.BlockSpec((1, tm, n), lambda i, t: (i, t, 0)),
        out_shape=jax.ShapeDtypeStruct((b, lt, n), F32),
        compiler_params=_cparams(("parallel", "parallel")),
        name="in_projection",
    )(xa, modtab, norm_w.reshape(1, d), w_bf16)


def _chunk_order(nc_ctx, nc_all, rev):
    if not rev:
        return lambda s: s
    return lambda s: jnp.where(s < nc_ctx, nc_ctx - 1 - s, nc_all - 1 - (s - nc_ctx))


def _dir_mask(mask_fwd, mask_rev):
    pieces = [jnp.broadcast_to(jnp.where(m, 1.0, 0.0), (N_HEADS,) + m.shape) for m in (mask_fwd, mask_rev)]
    return jnp.concatenate(pieces, axis=0)


def _tri01(n, rev):
    r = lax.broadcasted_iota(jnp.int32, (n, n), 0)
    c = lax.broadcasted_iota(jnp.int32, (n, n), 1)
    keep = (c >= r) if rev else (c <= r)
    return jnp.where(keep, 1.0, 0.0).astype(BF16)


def _level_ref_rows(g_ref, lo, hi, half, rev):
    seg = 2 * half
    nseg = CHUNK // seg
    nb = hi - lo
    row_of = lambda s: s * seg + half - (0 if rev else 1)
    if seg >= SUBLANES:
        pieces = [jnp.broadcast_to(g_ref[lo:hi, pl.ds(row_of(s), 1), :], (nb, seg, LANES)) for s in range(nseg)]
        return jnp.concatenate(pieces, axis=1)
    sub = lax.broadcasted_iota(jnp.int32, (nb, SUBLANES, LANES), 1) >> int(math.log2(seg))
    per = SUBLANES // seg
    pieces = []
    for v in range(CHUNK // SUBLANES):
        acc = jnp.broadcast_to(g_ref[lo:hi, pl.ds(row_of(v * per), 1), :], (nb, SUBLANES, LANES))
        for j in range(1, per):
            cand = jnp.broadcast_to(g_ref[lo:hi, pl.ds(row_of(v * per + j), 1), :], (nb, SUBLANES, LANES))
            acc = jnp.where(sub >= j, cand, acc)
        pieces.append(acc)
    return jnp.concatenate(pieces, axis=1)


def _hgrn_kernel(qf_ref, vf_ref, ff_ref, qb_ref, vb_ref, fb_ref, lbf_ref, lbb_ref, of_ref, ob_ref, st_ref, g_ref):
    @pl.when(pl.program_id(1) == 0)
    def _():
        st_ref[...] = jnp.zeros_like(st_ref)

    c = CHUNK
    nb = 2 * N_HEADS
    heads = lambda ref, lead: [ref[lead, :, hd * HEAD_DIM:(hd + 1) * HEAD_DIM] for hd in range(N_HEADS)]
    q = jnp.stack(heads(qf_ref, 0) + heads(qb_ref, 0))
    v = jnp.stack(heads(vf_ref, 0) + heads(vb_ref, 0))
    fr = jnp.stack(heads(ff_ref, 0) + heads(fb_ref, 0))
    lb = jnp.stack([lbf_ref[:, hd * HEAD_DIM:(hd + 1) * HEAD_DIM] for hd in range(N_HEADS)]
                   + [lbb_ref[:, hd * HEAD_DIM:(hd + 1) * HEAD_DIM] for hd in range(N_HEADS)])
    f = lb + (1.0 - lb) * _sigmoid(fr)
    k = 1.0 - f

    ri = lax.broadcasted_iota(jnp.int32, (c, c), 0)
    ci = lax.broadcasted_iota(jnp.int32, (c, c), 1)
    tri = _dir_mask(ci <= ri, ci >= ri).astype(BF16)
    g = None
    for part in _split_bf16(jnp.log(f), 3):
        t = lax.dot_general(tri, part, BMM_DIMS, preferred_element_type=F32)
        g = t if g is None else g + t
    g_ref[...] = g

    a = jnp.sum(q * k, axis=-1, keepdims=True) * jnp.where(ri == ci, 1.0, 0.0)
    for lv in range(int(math.log2(c))):
        half = 1 << lv
        r = jnp.concatenate([_level_ref_rows(g_ref, 0, N_HEADS, half, False),
                             _level_ref_rows(g_ref, N_HEADS, nb, half, True)], axis=0)
        e = jnp.exp(-jnp.abs(g - r))
        same = (ri >> (lv + 1)) == (ci >> (lv + 1))
        r_up = (ri & half) != 0
        c_up = (ci & half) != 0
        keep = _dir_mask(jnp.logical_and(same, jnp.logical_and(r_up, jnp.logical_not(c_up))),
                         jnp.logical_and(same, jnp.logical_and(jnp.logical_not(r_up), c_up)))
        a = a + keep * _bmm(q * e, k * e, BMM_NT_DIMS)
    g_tot = jnp.concatenate([g_ref[0:N_HEADS, pl.ds(c - 1, 1), :], g_ref[N_HEADS:nb, pl.ds(0, 1), :]], axis=0)
    st = st_ref[...]
    o = _bmm(a, v) + _bmm(q * jnp.exp(g), st, BMM_NT_DIMS)
    st_ref[...] = st * jnp.exp(g_tot) + _bmm(v, k * jnp.exp(g_tot - g), BMM_TN_DIMS)
    for hd in range(N_HEADS):
        of_ref[0, :, hd * HEAD_DIM:(hd + 1) * HEAD_DIM] = o[hd]
        ob_ref[0, :, hd * HEAD_DIM:(hd + 1) * HEAD_DIM] = o[N_HEADS + hd]


def _hgrn_scan(px, lb_f, lb_b, nc_ctx):
    b, lt, _ = px.shape
    nc = lt // CHUNK
    w = MIX_HALF
    order_b = _chunk_order(nc_ctx, nc, True)
    fwd = lambda blk: pl.BlockSpec((1, CHUNK, w), lambda i, s: (i, s, blk))
    bwd = lambda blk: pl.BlockSpec((1, CHUNK, w), lambda i, s: (i, order_b(s), blk))
    const = pl.BlockSpec((1, w), lambda i, s: (0, 0))
    return pl.pallas_call(
        _hgrn_kernel,
        grid=(b, nc),
        in_specs=[fwd(2), fwd(3), fwd(4), bwd(2), bwd(3), bwd(5), const, const],
        out_specs=[fwd(0), bwd(0)],
        out_shape=[jax.ShapeDtypeStruct((b, lt, w), F32), jax.ShapeDtypeStruct((b, lt, w), F32)],
        scratch_shapes=[pltpu.VMEM((2 * N_HEADS, HEAD_DIM, HEAD_DIM), F32),
                        pltpu.VMEM((2 * N_HEADS, CHUNK, HEAD_DIM), F32)],
        compiler_params=_cparams(("parallel", "arbitrary")),
        name="hgrn_scan",
    )(px, px, px, px, px, px, lb_f, lb_b)


def _even_merge_kernel(pa_ref, pg_ref, of_ref, ob_ref, x_ref, mod_ref, gnw_ref, ws_ref, bsb_ref,
                       hnw_ref, wo_ref, o_ref, cat_ref):
    tm = pa_ref.shape[1]
    z = _gelu_tanh(pa_ref[0])
    u = z[:, :MIX_HALF]
    v = _rms(z[:, MIX_HALF:], gnw_ref[...])
    for cc in range(tm // CHUNK):
        rs = slice(cc * CHUNK, (cc + 1) * CHUNK)
        for g in range(N_HEADS):
            cs = slice(g * HEAD_DIM, (g + 1) * HEAD_DIM)
            s = _bdot(ws_ref[g], v[rs, cs]) + bsb_ref[g]
            cat_ref[rs, cs] = (u[rs, cs] * s).astype(BF16)
    o = of_ref[0] + ob_ref[0]
    gate = _silu(pg_ref[0])
    for hd in range(N_HEADS):
        cs = slice(hd * HEAD_DIM, (hd + 1) * HEAD_DIM)
        rec = _rms(o[:, cs], hnw_ref[:, cs]) * gate[:, cs]
        cat_ref[:, MIX_HALF + hd * HEAD_DIM:MIX_HALF + (hd + 1) * HEAD_DIM] = rec.astype(BF16)
    y = jnp.dot(cat_ref[...], wo_ref[...], preferred_element_type=F32)
    o_ref[0] = x_ref[0] + mod_ref[0, 0][2:3] * y


def _even_merge(px, o_f, o_b, xa, modtab, gnw, ws_bf16, bsb, hnw, wo_bf16, nct):
    b, lt, d = xa.shape
    tm = TOKEN_TILE
    w = MIX_HALF
    tile = lambda width, blk: pl.BlockSpec((1, tm, width), lambda i, t: (i, t, blk))
    const = lambda shape: pl.BlockSpec(shape, lambda i, t: (0,) * len(shape))
    return pl.pallas_call(
        _even_merge_kernel,
        grid=(b, lt // tm),
        in_specs=[
            tile(2 * w, 0), tile(w, 6), tile(w, 0), tile(w, 0), tile(d, 0),
            pl.BlockSpec((1, 1, 8, d), _mod_index(nct)),
            const((1, w)), const((N_HEADS, CHUNK, CHUNK)), const((N_HEADS, CHUNK, HEAD_DIM)),
            const((1, w)), const((2 * w, d)),
        ],
        out_specs=tile(d, 0),
        out_shape=jax.ShapeDtypeStruct((b, lt, d), F32),
        scratch_shapes=[pltpu.VMEM((tm, 2 * w), BF16)],
        input_output_aliases={4: 0},
        compiler_params=_cparams(("parallel", "parallel")),
        name="even_merge",
    )(px, px, o_f, o_b, xa, modtab, gnw, ws_bf16, bsb, hnw, wo_bf16)


def _odd_prep_kernel(p_ref, prev_ref, next_ref, cw_ref, gvec_ref, qnw_ref, wq_ref, qkw_ref, kvnw_ref,
                     wkv_ref, kkw_ref, ct_ref, st_ref,
                     qkv_ref, gates_ref, q_ref, k_ref, v_ref, *, nct, ntiles, scale):
    tm = p_ref.shape[1]
    t = pl.program_id(1)
    w3 = 3 * MIX_HALF
    first = jnp.logical_or(t == 0, t == nct)
    last = jnp.logical_or(t == nct - 1, t == ntiles - 1)
    xin = p_ref[0, :, :w3]
    prev8 = jnp.where(first, 0.0, prev_ref[0])
    next8 = jnp.where(last, 0.0, next_ref[0])
    row8 = lax.broadcasted_iota(jnp.int32, (SUBLANES, w3), 0)
    pad = (GDN_CONV - 1) // 2
    acc = xin * cw_ref[pad:pad + 1, :]
    for sh in range(1, pad + 1):
        dn = pltpu.roll(xin, sh, 0)
        top = jnp.where(row8 < sh, pltpu.roll(prev8, sh, 0), dn[:SUBLANES])
        dn = jnp.concatenate([top, dn[SUBLANES:]], axis=0)
        acc = acc + dn * cw_ref[pad - sh:pad - sh + 1, :]
        up = pltpu.roll(xin, tm - sh, 0)
        bot = jnp.where(row8 >= SUBLANES - sh, pltpu.roll(next8, SUBLANES - sh, 0), up[tm - SUBLANES:])
        up = jnp.concatenate([up[:tm - SUBLANES], bot], axis=0)
        acc = acc + up * cw_ref[pad + sh:pad + sh + 1, :]
    act = _silu(acc)
    for hd in range(N_HEADS):
        for part, mul in ((0, HEAD_DIM ** -0.5), (1, 1.0)):
            cs = slice(part * MIX_HALF + hd * HEAD_DIM, part * MIX_HALF + (hd + 1) * HEAD_DIM)
            a = act[:, cs]
            qkv_ref[0, :, cs] = a * lax.rsqrt(jnp.sum(a * a, axis=-1, keepdims=True) + EPS) * mul
    qkv_ref[0, :, 2 * MIX_HALF:] = act[:, 2 * MIX_HALF:]

    gx = p_ref[0, :, 21 * LANES:22 * LANES]
    lane = lax.broadcasted_iota(jnp.int32, (tm, LANES), 1)
    a_neg = -jnp.exp(gvec_ref[0:1, :])
    gates_ref[0] = jnp.where(lane < 2 * N_HEADS, _sigmoid(gx), a_neg * _softplus(gx + gvec_ref[1:2, :]))

    ct = ct_ref[...]
    st = st_ref[...]
    inv_n = 1.0 / MLA_QK
    cq = _rms(p_ref[0, :, 16 * LANES:18 * LANES], qnw_ref[...])
    qraw = jnp.dot(cq.astype(BF16), wq_ref[...], preferred_element_type=F32)
    hw = MLA_HEADS * LANES
    for hd in range(MLA_HEADS):
        qa = qraw[:, hd * LANES:(hd + 1) * LANES]
        qs = qraw[:, hw + hd * LANES:hw + (hd + 1) * LANES]
        rs = lax.rsqrt(jnp.sum(qa * qa, axis=-1, keepdims=True) * inv_n + EPS)
        rot = (qa * rs * qkw_ref[0:1, :]) * ct + (qs * rs * qkw_ref[1:2, :]) * st
        q_ref[0, hd] = (rot * scale).astype(BF16)
    ckv = _rms(p_ref[0, :, 18 * LANES:19 * LANES], kvnw_ref[...])
    kv = jnp.dot(ckv.astype(BF16), wkv_ref[...], preferred_element_type=F32)
    kr = p_ref[0, :, 19 * LANES:20 * LANES]
    krs = p_ref[0, :, 20 * LANES:21 * LANES]
    for hd in range(MLA_HEADS):
        ka = kv[:, hd * LANES:(hd + 1) * LANES] + kr
        rs = lax.rsqrt(jnp.sum(ka * ka, axis=-1, keepdims=True) * inv_n + EPS)
        rot = (ka * rs * kkw_ref[0:1, :]) * ct + (krs * rs * kkw_ref[1:2, :]) * st
        k_ref[0, hd] = rot.astype(BF16)
        v_ref[0, hd] = jnp.where(lane < MLA_V, kv[:, hw + hd * LANES:hw + (hd + 1) * LANES], 1.0).astype(BF16)


def _odd_prep(px, cw, gvec, qnw, wq, qkw, kvnw, wkv, kkw, ctab, stab, nct):
    b, lt, n = px.shape
    tm = TOKEN_TILE
    ntiles = lt // tm
    w3 = 3 * MIX_HALF
    per = tm // SUBLANES
    nrow8 = lt // SUBLANES
    const = lambda shape: pl.BlockSpec(shape, lambda i, t: (0,) * len(shape))
    head_tile = pl.BlockSpec((1, MLA_HEADS, tm, LANES), lambda i, t: (i, 0, t, 0))
    head_shape = jax.ShapeDtypeStruct((b, MLA_HEADS, lt, LANES), BF16)
    q_tile = pl.BlockSpec((1, MLA_HEADS, tm, LANES), lambda i, t: (i, 0, jnp.maximum(t - nct, 0), 0))
    q_shape = jax.ShapeDtypeStruct((b, MLA_HEADS, lt - nct * tm, LANES), BF16)
    return pl.pallas_call(
        functools.partial(_odd_prep_kernel, nct=nct, ntiles=ntiles, scale=MLA_QK ** -0.5 * math.log2(math.e)),
        grid=(b, ntiles),
        in_specs=[
            pl.BlockSpec((1, tm, n), lambda i, t: (i, t, 0)),
            pl.BlockSpec((1, SUBLANES, w3), lambda i, t: (i, jnp.maximum(t * per - 1, 0), 0)),
            pl.BlockSpec((1, SUBLANES, w3), lambda i, t: (i, jnp.minimum((t + 1) * per, nrow8 - 1), 0)),
            const((8, w3)), const((8, LANES)), const((1, MLA_Q_RANK)),
            const((MLA_Q_RANK, 2 * MLA_HEADS * LANES)), const((8, LANES)), const((1, MLA_KV_RANK)),
            const((MLA_KV_RANK, 2 * MLA_HEADS * LANES)), const((8, LANES)),
            pl.BlockSpec((tm, LANES), lambda i, t: (t, 0)),
            pl.BlockSpec((tm, LANES), lambda i, t: (t, 0)),
        ],
        out_specs=[
            pl.BlockSpec((1, tm, w3), lambda i, t: (i, t, 0)),
            pl.BlockSpec((1, tm, LANES), lambda i, t: (i, t, 0)),
            q_tile, head_tile, head_tile,
        ],
        out_shape=[
            jax.ShapeDtypeStruct((b, lt, w3), F32),
            jax.ShapeDtypeStruct((b, lt, LANES), F32),
            q_shape, head_shape, head_shape,
        ],
        compiler_params=_cparams(("parallel", "arbitrary")),
        name="odd_prep",
    )(px, px, px, cw, gvec, qnw, wq, qkw, kvnw, wkv, kkw, ctab, stab)


def _gdn_kernel(qf_ref, kf_ref, vf_ref, gf_ref, qb_ref, kb_ref, vb_ref, gb_ref, of_ref, ob_ref, s_ref):
    @pl.when(pl.program_id(1) == 0)
    def _():
        s_ref[...] = jnp.zeros_like(s_ref)

    c = CHUNK
    nb = 2 * N_HEADS
    heads = lambda ref: [ref[0, :, hd * HEAD_DIM:(hd + 1) * HEAD_DIM] for hd in range(N_HEADS)]
    q = jnp.stack(heads(qf_ref) + heads(qb_ref))
    k = jnp.stack(heads(kf_ref) + heads(kb_ref))
    v = jnp.stack(heads(vf_ref) + heads(vb_ref))
    gf = gf_ref[0]
    gb = gb_ref[0]
    lane_col = lambda gt, j: jnp.broadcast_to(gt[:, j:j + 1], (c, LANES))
    beta = jnp.stack([lane_col(gf, hd) for hd in range(N_HEADS)]
                     + [lane_col(gb, N_HEADS + hd) for hd in range(N_HEADS)])
    la = jnp.stack([lane_col(gf, 2 * N_HEADS + hd) for hd in range(N_HEADS)]
                   + [lane_col(gb, 3 * N_HEADS + hd) for hd in range(N_HEADS)])

    ri = lax.broadcasted_iota(jnp.int32, (c, c), 0)
    ci = lax.broadcasted_iota(jnp.int32, (c, c), 1)
    incl = _dir_mask(ci <= ri, ci >= ri)
    strict = _dir_mask(ci < ri, ci > ri)

    tri = incl.astype(BF16)
    g = None
    for part in _split_bf16(la, 3):
        t = lax.dot_general(tri, part, BMM_DIMS, preferred_element_type=F32)
        g = t if g is None else g + t
    gamma = incl * jnp.exp(jnp.minimum(g - jnp.swapaxes(g, 1, 2), 0.0))
    kq = _bmm(jnp.concatenate([k, q], axis=1), k, BMM_NT_DIMS)
    nm = strict * beta * kq[:, :c] * gamma

    x = None
    for lv in range(int(math.log2(c))):
        same = (ri >> (lv + 1)) == (ci >> (lv + 1))
        r_up = ((ri >> lv) & 1) == 1
        c_up = ((ci >> lv) & 1) == 1
        blk = nm * _dir_mask(jnp.logical_and(same, jnp.logical_and(r_up, jnp.logical_not(c_up))),
                             jnp.logical_and(same, jnp.logical_and(jnp.logical_not(r_up), c_up)))
        if x is None:
            x = jnp.where(ri == ci, 1.0, 0.0) - blk
        else:
            x = x - _bmm(x, _bmm(blk, x))

    eg = jnp.exp(g)
    uw = _bmm(x, jnp.concatenate([beta * v, beta * k * eg], axis=2))
    qk = kq[:, c:] * gamma
    g_tot = jnp.concatenate([g[:N_HEADS, c - 1:c, :], g[N_HEADS:, 0:1, :]], axis=0)
    s = s_ref[...]
    v_new = uw[:, :, :HEAD_DIM] - _bmm(uw[:, :, HEAD_DIM:], s)
    o = _bmm(q * eg, s) + _bmm(qk, v_new)
    s_ref[...] = jnp.exp(g_tot) * s + _bmm(k * jnp.exp(g_tot - g), v_new, BMM_TN_DIMS)
    for hd in range(N_HEADS):
        of_ref[0, :, hd * HEAD_DIM:(hd + 1) * HEAD_DIM] = o[hd]
        ob_ref[0, :, hd * HEAD_DIM:(hd + 1) * HEAD_DIM] = o[N_HEADS + hd]


def _gdn_scan(qkv, gates, nc_ctx):
    b, lt, _ = qkv.shape
    nc = lt // CHUNK
    w = MIX_HALF
    specs = []
    for rev in (False, True):
        order = _chunk_order(nc_ctx, nc, rev)
        specs += [pl.BlockSpec((1, CHUNK, w), lambda i, s, o=order, blk=blk: (i, o(s), blk)) for blk in range(3)]
        specs += [pl.BlockSpec((1, CHUNK, LANES), lambda i, s, o=order: (i, o(s), 0))]
    order_b = _chunk_order(nc_ctx, nc, True)
    return pl.pallas_call(
        _gdn_kernel,
        grid=(b, nc),
        in_specs=specs,
        out_specs=[pl.BlockSpec((1, CHUNK, w), lambda i, s: (i, s, 0)),
                   pl.BlockSpec((1, CHUNK, w), lambda i, s: (i, order_b(s), 0))],
        out_shape=[jax.ShapeDtypeStruct((b, lt, w), F32), jax.ShapeDtypeStruct((b, lt, w), F32)],
        scratch_shapes=[pltpu.VMEM((2 * N_HEADS, HEAD_DIM, HEAD_DIM), F32)],
        compiler_params=_cparams(("parallel", "arbitrary")),
        name="gdn_scan",
    )(qkv, qkv, qkv, gates, qkv, qkv, qkv, gates)


def _attn_kernel(q_ref, k_ref, v_ref, o_ref, m_ref, acc_ref, s_ref, *, tk, nk):
    m_ref[...] = jnp.full_like(m_ref, -jnp.inf)
    acc_ref[...] = jnp.zeros_like(acc_ref)

    def tile(j):
        return pl.ds(pl.multiple_of(j * tk, tk), tk)

    def scores(j, slot):
        for a in range(2):
            s_ref[slot, a] = lax.dot_general(q_ref[0, a], k_ref[0, a, tile(j), :], NT_DIMS,
                                             preferred_element_type=F32)

    def softmax_pv(j, slot):
        for a in range(2):
            s = s_ref[slot, a]
            m_prev = m_ref[a]
            m_new = jnp.maximum(m_prev, jnp.max(s, axis=-1, keepdims=True))
            p = jnp.exp2(s - m_new[:, :1]).astype(BF16)
            acc_ref[a] = jnp.exp2(m_prev - m_new) * acc_ref[a] + jnp.dot(
                p, v_ref[0, a, tile(j), :], preferred_element_type=F32)
            m_ref[a] = m_new

    scores(0, 0)

    def pair(i, carry):
        scores(2 * i + 1, 1)
        softmax_pv(2 * i, 0)
        scores(jnp.minimum(2 * i + 2, nk - 1), 0)
        softmax_pv(2 * i + 1, 1)
        return carry

    lax.fori_loop(0, nk // 2, pair, 0)
    if nk % 2:
        softmax_pv(nk - 1, 0)
    lane = lax.broadcasted_iota(jnp.int32, acc_ref.shape[1:], 1)
    a0 = acc_ref[0]
    a1 = acc_ref[1]
    o_ref[0] = jnp.where(lane < MLA_V, a0 / pltpu.roll(a0, MLA_V, 1), pltpu.roll(a1, MLA_V, 1) / a1)


def _attention(q, k, v):
    b, nh, lq, _ = q.shape
    lt = k.shape[2]
    tq = next(c for c in ATTN_TQ_CANDIDATES if lq % c == 0)
    tk = next(c for c in ATTN_TK_CANDIDATES if lt % c == 0)
    return pl.pallas_call(
        functools.partial(_attn_kernel, tk=tk, nk=lt // tk),
        grid=(b, nh // 2, lq // tq),
        in_specs=[
            pl.BlockSpec((1, 2, tq, LANES), lambda i, h, qi: (i, h, qi, 0)),
            pl.BlockSpec((1, 2, lt, LANES), lambda i, h, qi: (i, h, 0, 0)),
            pl.BlockSpec((1, 2, lt, LANES), lambda i, h, qi: (i, h, 0, 0)),
        ],
        out_specs=pl.BlockSpec((1, tq, 2 * MLA_V), lambda i, h, qi: (i, qi, h)),
        out_shape=jax.ShapeDtypeStruct((b, lq, nh * MLA_V), F32),
        scratch_shapes=[pltpu.VMEM((2, tq, LANES), F32), pltpu.VMEM((2, tq, LANES), F32),
                        pltpu.VMEM((2, 2, tq, tk), F32)],
        compiler_params=_cparams(("parallel", "parallel", "arbitrary")),
        name="mla_attention",
    )(q, k, v)


def _odd_merge_kernel(z_ref, of_ref, ob_ref, att_ref, x_ref, mod_ref, gnw_ref, wo_ref, o_ref, cat_ref):
    o = of_ref[0] + ob_ref[0]
    gate = _silu(z_ref[0])
    for hd in range(N_HEADS):
        cs = slice(hd * HEAD_DIM, (hd + 1) * HEAD_DIM)
        cat_ref[:, cs] = (_rms(o[:, cs], gnw_ref[...]) * gate[:, cs]).astype(BF16)
    cat_ref[:, MIX_HALF:] = att_ref[0].astype(BF16)
    y = jnp.dot(cat_ref[...], wo_ref[...], preferred_element_type=F32)
    o_ref[0] = x_ref[0] + mod_ref[0, 0][2:3] * y


def _odd_merge(px, o_f, o_b, att, xa, modtab, gnw, wo_bf16, nct):
    b, lt, d = xa.shape
    tm = TOKEN_TILE
    lq = att.shape[1]
    w = MIX_HALF
    lat = lambda width, blk: pl.BlockSpec((1, tm, width), lambda i, t: (i, t + nct, blk))
    const = lambda shape: pl.BlockSpec(shape, lambda i, t: (0,) * len(shape))
    return pl.pallas_call(
        _odd_merge_kernel,
        grid=(b, lq // tm),
        in_specs=[
            lat(w, 3), lat(w, 0), lat(w, 0),
            pl.BlockSpec((1, tm, w), lambda i, t: (i, t, 0)),
            lat(d, 0),
            pl.BlockSpec((1, 1, 8, d), lambda i, t: (i, 1, 0, 0)),
            const((1, HEAD_DIM)), const((2 * w, d)),
        ],
        out_specs=lat(d, 0),
        out_shape=jax.ShapeDtypeStruct((b, lt, d), F32),
        scratch_shapes=[pltpu.VMEM((tm, 2 * w), BF16)],
        input_output_aliases={4: 0},
        compiler_params=_cparams(("parallel", "parallel")),
        name="odd_merge",
    )(px, o_f, o_b, att, xa, modtab, gnw, wo_bf16)


def _router_kernel(x_ref, mod_ref, nw_ref, wr_ref, br_ref, h_ref, info_ref, cnt_ref, run_ref):
    @pl.when(jnp.logical_and(pl.program_id(0) == 0, pl.program_id(1) == 0))
    def _():
        run_ref[...] = jnp.zeros_like(run_ref)

    tm = x_ref.shape[1]
    mod = mod_ref[0, 0]
    h = _rms(x_ref[0], nw_ref[...]) * (1.0 + mod[4:5]) + mod[3:4]
    h_ref[0] = h
    logits = _dot3(h, wr_ref[...]) + br_ref[...]
    lane = lax.broadcasted_iota(jnp.int32, (tm, LANES), 1)
    lane_f = lane.astype(F32)
    big = float(LANES)
    neg = -jnp.inf
    is_g = jnp.logical_and(lane >= N_EXPERTS, lane < N_EXPERTS + N_GROUPS)
    lg = jnp.where(is_g, logits, neg)
    mg = jnp.max(lg, axis=-1, keepdims=True)
    pg_top = 1.0 / jnp.sum(jnp.exp(lg - mg), axis=-1, keepdims=True)
    g_idx = jnp.min(jnp.where(lg == mg, lane_f - N_EXPERTS, big), axis=-1, keepdims=True)
    in_g = jnp.logical_and(lane < N_EXPERTS, (lane >> 3).astype(F32) == g_idx)
    le = jnp.where(in_g, logits, neg)
    m1 = jnp.max(le, axis=-1, keepdims=True)
    e1 = jnp.min(jnp.where(le == m1, lane_f, big), axis=-1, keepdims=True)
    le2 = jnp.where(lane_f == e1, neg, le)
    m2 = jnp.max(le2, axis=-1, keepdims=True)
    e2 = jnp.min(jnp.where(le2 == m2, lane_f, big), axis=-1, keepdims=True)
    se = jnp.sum(jnp.exp(le - m1), axis=-1, keepdims=True)
    p1 = 1.0 / se
    p2 = jnp.exp(m2 - m1) / se
    w1 = pg_top * p1 / (p1 + p2)
    w2 = pg_top * p2 / (p1 + p2)
    hit1 = lane_f == e1
    hit2 = lane_f == e2
    oh = jnp.where(jnp.logical_or(hit1, hit2), 1.0, 0.0)
    r = lax.broadcasted_iota(jnp.int32, (tm, tm), 0)
    c = lax.broadcasted_iota(jnp.int32, (tm, tm), 1)
    before = jnp.dot(jnp.where(c < r, 1.0, 0.0).astype(BF16), oh.astype(BF16), preferred_element_type=F32)
    base = before + run_ref[0:1, :]
    pos1 = jnp.sum(jnp.where(hit1, base, 0.0), axis=-1, keepdims=True)
    pos2 = jnp.sum(jnp.where(hit2, base, 0.0), axis=-1, keepdims=True)
    run = run_ref[0:1, :] + jnp.sum(oh, axis=0, keepdims=True)
    run_ref[...] = jnp.broadcast_to(run, run_ref.shape)
    cnt_ref[...] = jnp.broadcast_to(run, cnt_ref.shape)
    info = jnp.zeros((tm, LANES), F32)
    for j, val in enumerate((e1, e2, pos1, pos2, w1, w2)):
        info = jnp.where(lane == j, val, info)
    info_ref[0] = info


def _router(xa, modtab, norm_w, wr, br, nct):
    b, lt, d = xa.shape
    tm = TOKEN_TILE
    const = lambda shape: pl.BlockSpec(shape, lambda i, t: (0,) * len(shape))
    return pl.pallas_call(
        _router_kernel,
        grid=(b, lt // tm),
        in_specs=[
            pl.BlockSpec((1, tm, d), lambda i, t: (i, t, 0)),
            pl.BlockSpec((1, 1, 8, d), _mod_index(nct)),
            const((1, d)), const((d, LANES)), const((1, LANES)),
        ],
        out_specs=[
            pl.BlockSpec((1, tm, d), lambda i, t: (i, t, 0)),
            pl.BlockSpec((1, tm, LANES), lambda i, t: (i, t, 0)),
            const((SUBLANES, LANES)),
        ],
        out_shape=[
            jax.ShapeDtypeStruct((b, lt, d), F32),
            jax.ShapeDtypeStruct((b, lt, LANES), F32),
            jax.ShapeDtypeStruct((SUBLANES, LANES), F32),
        ],
        scratch_shapes=[pltpu.VMEM((SUBLANES, LANES), F32)],
        compiler_params=_cparams(("arbitrary", "arbitrary")),
        name="moe_router",
    )(xa, modtab, norm_w.reshape(1, d), wr, br)


def _row_dma_loops(n_rows, row_copy):
    def issue(i, carry):
        base = pl.multiple_of(i * SUBLANES, SUBLANES)
        for u in range(SUBLANES):
            row_copy(base + u, 0).start(priority=0)
            row_copy(base + u, 1).start(priority=1)
        return carry

    def drain(i, carry):
        base = pl.multiple_of(i * SUBLANES, SUBLANES)
        for u in range(SUBLANES):
            row_copy(base + u, 0).wait()
            row_copy(base + u, 1).wait()
        return carry

    lax.fori_loop(0, n_rows // SUBLANES, issue, 0, unroll=ROW_DMA_UNROLL)
    lax.fori_loop(0, n_rows // SUBLANES, drain, 0, unroll=ROW_DMA_UNROLL)


def _dispatch_kernel(pad_ref, dest_ref, h_ref, xs_ref, zero_ref, sem):
    tm = h_ref.shape[0]

    @pl.when(pl.program_id(0) == 0)
    def _():
        zero_ref[...] = jnp.zeros_like(zero_ref)
        for e in range(pad_ref.shape[0]):
            start = pl.multiple_of(pad_ref[e], SUBLANES)
            window = pltpu.make_async_copy(zero_ref, xs_ref.at[pl.ds(start, MOE_BLOCK)], sem)
            window.start()
            window.wait()

    def row_copy(r, k):
        return pltpu.make_async_copy(h_ref.at[pl.ds(r, 1)], xs_ref.at[pl.ds(dest_ref[0, 0, 2 * r + k], 1)], sem)

    _row_dma_loops(tm, row_copy)


def _dispatch(pad_start, dest3, h2d, n_slots):
    t, d = h2d.shape
    tm = TOKEN_TILE
    grid_spec = pltpu.PrefetchScalarGridSpec(
        num_scalar_prefetch=1,
        grid=(t // tm,),
        in_specs=[
            pl.BlockSpec((1, 1, 2 * tm), lambda i, pad: (i, 0, 0), memory_space=pltpu.SMEM),
            pl.BlockSpec((tm, d), lambda i, pad: (i, 0)),
        ],
        out_specs=pl.BlockSpec(memory_space=pl.ANY),
        scratch_shapes=[pltpu.VMEM((MOE_BLOCK, d), F32), pltpu.SemaphoreType.DMA],
    )
    return pl.pallas_call(
        _dispatch_kernel,
        grid_spec=grid_spec,
        out_shape=jax.ShapeDtypeStruct((n_slots, d), F32),
        compiler_params=_cparams(("arbitrary",)),
        name="moe_dispatch",
    )(pad_start, dest3, h2d)


def _expert_kernel(be_ref, nu_ref, xs_ref, wg_ref, wu_ref, wd_ref, ys_ref):
    i = pl.program_id(0)

    @pl.when(i < nu_ref[0])
    def _():
        x = xs_ref[...].astype(BF16)
        a = jnp.dot(x, wg_ref[0].astype(BF16), preferred_element_type=F32)
        u = jnp.dot(x, wu_ref[0].astype(BF16), preferred_element_type=F32)
        ys_ref[...] = jnp.dot((_silu(a) * u).astype(BF16), wd_ref[0].astype(BF16), preferred_element_type=F32)

    @pl.when(i >= nu_ref[0])
    def _():
        ys_ref[...] = jnp.zeros_like(ys_ref)


def _experts(blk_e, n_used, xs, wg, wu, wd):
    n_slots, d = xs.shape
    ff = wg.shape[2]
    nb = n_slots // MOE_BLOCK
    grid_spec = pltpu.PrefetchScalarGridSpec(
        num_scalar_prefetch=2,
        grid=(nb,),
        in_specs=[
            pl.BlockSpec((MOE_BLOCK, d), lambda i, be, nu: (jnp.minimum(i, jnp.maximum(nu[0] - 1, 0)), 0)),
            pl.BlockSpec((1, d, ff), lambda i, be, nu: (be[i], 0, 0)),
            pl.BlockSpec((1, d, ff), lambda i, be, nu: (be[i], 0, 0)),
            pl.BlockSpec((1, ff, d), lambda i, be, nu: (be[i], 0, 0)),
        ],
        out_specs=pl.BlockSpec((MOE_BLOCK, d), lambda i, be, nu: (i, 0)),
    )
    return pl.pallas_call(
        _expert_kernel,
        grid_spec=grid_spec,
        out_shape=jax.ShapeDtypeStruct((n_slots, d), F32),
        compiler_params=_cparams(("arbitrary",)),
        name="moe_experts",
    )(blk_e, n_used, xs, wg, wu, wd)


def _combine_kernel(dest_ref, info_ref, x_ref, mod_ref, ys_ref, o_ref, buf_ref, sem):
    tm = x_ref.shape[1]

    def row_copy(r, k):
        return pltpu.make_async_copy(ys_ref.at[pl.ds(dest_ref[0, 0, 2 * r + k], 1)],
                                     buf_ref.at[k, pl.ds(r, 1)], sem)

    _row_dma_loops(tm, row_copy)
    info = info_ref[0]
    y = info[:, 4:5] * buf_ref[0] + info[:, 5:6] * buf_ref[1]
    o_ref[0] = x_ref[0] + mod_ref[0, 0][5:6] * y


def _combine(dest3, info, xa, modtab, ys, nct, latent_only):
    b, lt, d = xa.shape
    tm = TOKEN_TILE
    nt = lt // tm
    skip = nct if latent_only else 0
    mod_map = _mod_index(nct)
    return pl.pallas_call(
        _combine_kernel,
        grid=(b, nt - skip),
        in_specs=[
            pl.BlockSpec((1, 1, 2 * tm), lambda i, t: (i * nt + t + skip, 0, 0), memory_space=pltpu.SMEM),
            pl.BlockSpec((1, tm, LANES), lambda i, t: (i, t + skip, 0)),
            pl.BlockSpec((1, tm, d), lambda i, t: (i, t + skip, 0)),
            pl.BlockSpec((1, 1, 8, d), lambda i, t: mod_map(i, t + skip)),
            pl.BlockSpec(memory_space=pl.ANY),
        ],
        out_specs=pl.BlockSpec((1, tm, d), lambda i, t: (i, t, 0)),
        out_shape=jax.ShapeDtypeStruct((b, lt - skip * tm, d), F32),
        scratch_shapes=[pltpu.VMEM((2, tm, d), F32), pltpu.SemaphoreType.DMA],
        input_output_aliases={} if latent_only else {2: 0},
        compiler_params=_cparams(("arbitrary", "arbitrary")),
        name="moe_combine",
    )(dest3, info, xa, modtab, ys)


def _moe(xa, modtab, norm_w, wg_r, bg_r, we_r, be_r, w_gate, w_up, w_down, nct, latent_only=False):
    b, lt, d = xa.shape
    t = b * lt
    tm = TOKEN_TILE
    pad = LANES - N_EXPERTS - N_GROUPS
    wr = jnp.concatenate([we_r, wg_r, jnp.zeros((d, pad), F32)], axis=1)
    br = jnp.concatenate([be_r, bg_r, jnp.zeros((pad,), F32)]).reshape(1, LANES)
    h, info, cnt = _router(xa, modtab, norm_w, wr, br, nct)
    counts = cnt[0, :N_EXPERTS].astype(jnp.int32)
    padded = (counts + MOE_BLOCK - 1) // MOE_BLOCK * MOE_BLOCK
    pend = jnp.cumsum(padded)
    pstart = pend - padded
    info2 = info.reshape(t, LANES)
    e_idx = info2[:, 0:2].astype(jnp.int32)
    first_slot = jnp.sum(jnp.where(e_idx[:, :, None] == jnp.arange(N_EXPERTS, dtype=jnp.int32), pstart, 0), axis=-1)
    dest = first_slot + info2[:, 2:4].astype(jnp.int32)
    dest3 = dest.reshape(t // tm, 1, 2 * tm)
    n_blk = -(-2 * t // MOE_BLOCK) + N_EXPERTS
    blk_start = jnp.arange(n_blk, dtype=jnp.int32) * MOE_BLOCK
    blk_e = jnp.minimum(jnp.sum((pend[None, :] <= blk_start[:, None]).astype(jnp.int32), axis=1), N_EXPERTS - 1)
    n_used = (pend[-1:] // MOE_BLOCK).astype(jnp.int32)
    n_slots = n_blk * MOE_BLOCK
    tail = (n_used[0] + jnp.arange(N_EXPERTS, dtype=jnp.int32)) * MOE_BLOCK
    pad_start = jnp.minimum(jnp.concatenate([(pstart + counts) // SUBLANES * SUBLANES, tail]),
                            n_slots - MOE_BLOCK).astype(jnp.int32)
    xs = _dispatch(pad_start, dest3, h.reshape(t, d), n_slots)
    ys = _experts(blk_e, n_used, xs, w_gate, w_up, w_down)
    return _combine(dest3, info, xa, modtab, ys, nct, latent_only)


_ROPE_SWAP = np.concatenate([np.arange(8, 16), np.arange(0, 8), np.arange(24, 32), np.arange(16, 24)])


def _odd_in_weights(w_in):
    d = w_in.shape[0]
    w4 = 4 * MIX_HALF
    o_beta = w4
    o_cq = o_beta + 4 * N_HEADS
    o_ckv = o_cq + MLA_Q_RANK
    o_kr = o_ckv + MLA_KV_RANK
    z = lambda n: jnp.zeros((d, n), F32)
    kr = w_in[:, o_kr:o_kr + MLA_ROPE]
    blocks = [
        w_in[:, :w4], w_in[:, o_cq:o_ckv], w_in[:, o_ckv:o_kr],
        z(MLA_NOPE), kr, z(LANES - MLA_QK),
        z(MLA_NOPE), kr[:, _ROPE_SWAP], z(LANES - MLA_QK),
        w_in[:, o_beta:o_cq], z(LANES - 4 * N_HEADS),
    ]
    return jnp.concatenate(blocks, axis=1)


def _per_head_lanes(w, per, take, width):
    r = w.shape[0]
    wh = w.reshape(r, -1, per)[:, :, take]
    return jnp.pad(wh, ((0, 0), (0, 0), (0, width - wh.shape[2]))).reshape(r, -1)


def _rope_lane_weights(w):
    plain = jnp.pad(w, (0, LANES - MLA_QK))
    sw = jnp.pad(jnp.concatenate([jnp.zeros((MLA_NOPE,), F32), w[MLA_NOPE:][_ROPE_SWAP]]), (0, LANES - MLA_QK))
    return jnp.pad(jnp.stack([plain, sw]), ((0, 6), (0, 0)))


def _rope_tables(n_ctx, n_lat):
    quarter = MLA_ROPE // 4
    freqs = ROPE_THETA ** (-jnp.arange(quarter, dtype=F32) / quarter)
    pos = jnp.arange(n_lat)
    row = (pos // GRID_W).astype(F32)[:, None] * freqs
    col = (pos % GRID_W).astype(F32)[:, None] * freqs
    ones = jnp.ones((n_lat, MLA_NOPE), F32)
    zpad = jnp.zeros((n_lat, LANES - MLA_QK), F32)
    ctab = jnp.concatenate([ones, jnp.cos(row), jnp.cos(row), jnp.cos(col), jnp.cos(col), zpad], axis=1)
    stab = jnp.concatenate([0 * ones, -jnp.sin(row), jnp.sin(row), -jnp.sin(col), jnp.sin(col), zpad], axis=1)
    c_ctx = jnp.concatenate([jnp.ones((n_ctx, MLA_QK), F32), jnp.zeros((n_ctx, LANES - MLA_QK), F32)], axis=1)
    return jnp.concatenate([c_ctx, ctab], axis=0), jnp.concatenate([jnp.zeros((n_ctx, LANES), F32), stab], axis=0)


def _lower_bound(logits, layer):
    return jnp.cumsum(jax.nn.softmax(logits.astype(F32), axis=0), axis=0)[layer]


def kernel(x, c, ctx, c_ctx, ada_w, ada_b, norm_mix_w, norm_ffn_w, even_w_in, even_w_out, gmlp_norm_w, gmlp_ws, gmlp_bs, hgrn_lb_logits, hgrn_norm_w, odd_w_in, odd_w_out, gdn_conv_w, gdn_a_log, gdn_dt_bias, gdn_norm_w, mla_q_norm_w, mla_wq_up, mla_kv_norm_w, mla_wkv_up, mla_qk_norm_q, mla_qk_norm_k, router_group_w, router_group_b, router_expert_w, router_expert_b, moe_w_gate, moe_w_up, moe_w_down):
    b, n_lat, d = x.shape
    n_ctx = ctx.shape[1]
    depth = ada_w.shape[0]
    tm = TOKEN_TILE
    assert depth == 2 and n_ctx % tm == 0 and n_lat % tm == 0 and n_lat % GRID_W == 0
    nct = n_ctx // tm
    ncc = n_ctx // CHUNK

    xa = jnp.concatenate([ctx, x], axis=1)
    rows = -(-(b + 1) // SUBLANES) * SUBLANES
    cvec = jnp.zeros((rows, d), F32).at[:b].set(c).at[b].set(c_ctx)
    mods = _mod_vectors(cvec, ada_w, ada_b).reshape(depth, rows, 6, d)

    def mod_table(layer):
        mx = mods[layer, :b]
        mc = jnp.broadcast_to(mods[layer, b], (b, 6, d))
        return jnp.pad(jnp.stack([mc, mx], axis=1), ((0, 0), (0, 0), (0, 2), (0, 0)))

    modtab = mod_table(0)
    px = _in_projection(xa, modtab, norm_mix_w[0], even_w_in[0].astype(BF16), nct)
    lb_f = _lower_bound(hgrn_lb_logits[0], 0).reshape(1, MIX_HALF)
    lb_b = _lower_bound(hgrn_lb_logits[1], 0).reshape(1, MIX_HALF)
    o_f, o_b = _hgrn_scan(px, lb_f, lb_b, ncc)
    bsb = jnp.broadcast_to(gmlp_bs[0][:, :, None], (N_HEADS, CHUNK, HEAD_DIM))
    xa = _even_merge(px, o_f, o_b, xa, modtab, gmlp_norm_w[0].reshape(1, MIX_HALF), gmlp_ws[0].astype(BF16), bsb,
                     hgrn_norm_w[0].reshape(1, MIX_HALF), even_w_out[0].astype(BF16), nct)
    xa = _moe(xa, modtab, norm_ffn_w[0], router_group_w[0], router_group_b[0], router_expert_w[0],
              router_expert_b[0], moe_w_gate[0], moe_w_up[0], moe_w_down[0], nct)

    modtab = mod_table(1)
    px = _in_projection(xa, modtab, norm_mix_w[1], _odd_in_weights(odd_w_in[0]).astype(BF16), nct)
    cw = jnp.pad(gdn_conv_w[0], ((0, 8 - GDN_CONV), (0, 0)))
    gvec = jnp.zeros((8, LANES), F32)
    gvec = gvec.at[0, 2 * N_HEADS:4 * N_HEADS].set(gdn_a_log[0].reshape(-1))
    gvec = gvec.at[1, 2 * N_HEADS:4 * N_HEADS].set(gdn_dt_bias[0].reshape(-1))
    wq = mla_wq_up[0]
    rope_cols = MLA_NOPE + _ROPE_SWAP
    wq_all = jnp.concatenate([
        _per_head_lanes(wq, MLA_QK, np.arange(MLA_QK), LANES),
        jnp.pad(_per_head_lanes(wq, MLA_QK, rope_cols, LANES - MLA_NOPE).reshape(MLA_Q_RANK, MLA_HEADS, -1),
                ((0, 0), (0, 0), (MLA_NOPE, 0)))[:, :, :LANES].reshape(MLA_Q_RANK, -1),
    ], axis=1).astype(BF16)
    wkv = mla_wkv_up[0]
    wkv_all = jnp.concatenate([
        _per_head_lanes(wkv, MLA_NOPE + MLA_V, np.arange(MLA_NOPE), LANES),
        _per_head_lanes(wkv, MLA_NOPE + MLA_V, MLA_NOPE + np.arange(MLA_V), LANES),
    ], axis=1).astype(BF16)
    ctab, stab = _rope_tables(n_ctx, n_lat)
    qkv, gates, q, k, v = _odd_prep(px, cw, gvec, mla_q_norm_w[0].reshape(1, -1), wq_all,
                                    _rope_lane_weights(mla_qk_norm_q[0]), mla_kv_norm_w[0].reshape(1, -1), wkv_all,
                                    _rope_lane_weights(mla_qk_norm_k[0]), ctab, stab, nct)
    o_f, o_b = _gdn_scan(qkv, gates, ncc)
    att = _attention(q, k, v)
    xa = _odd_merge(px, o_f, o_b, att, xa, modtab, gdn_norm_w[0].reshape(1, HEAD_DIM), odd_w_out[0].astype(BF16), nct)
    return _moe(xa, modtab, norm_ffn_w[1], router_group_w[1], router_group_b[1], router_expert_w[1],
                router_expert_b[1], moe_w_gate[1], moe_w_up[1], moe_w_down[1], nct, latent_only=True)
```
